```python
import jax, jax.numpy as jnp
from jax import lax
import numpy as np


D_MODEL = 1024
BATCH = 8
SEQ = 4096
DEPTH = 2

HEAD_DIM = 64
FOX_HEADS = 4
DIL_HEADS = 4
DIL_PATTERNS = ((128, 1), (512, 4), (2048, 16))
RET_HEADS = 4
RET_CHUNK = 128
MLA_HEADS = 4
MLA_Q_RANK = 256
MLA_KV_RANK = 128
MLA_NOPE = 64
MLA_ROPE = 32
MLA_V = 64
ROPE_THETA = 10000.0
Q_BLOCK = 128
N_BRANCH = 4
BRANCH_WIDTH = 4 * HEAD_DIM
D_FF = ((8 * D_MODEL + 3 * 256 - 1) // (3 * 256)) * 256
RMS_EPS = 1e-6
GN_EPS = 1e-5

FOX_W = FOX_HEADS * HEAD_DIM
DIL_W = DIL_HEADS * HEAD_DIM
RET_W = RET_HEADS * HEAD_DIM
IN_SIZES = (FOX_W, FOX_W, FOX_W, FOX_HEADS,
            DIL_W, DIL_W, DIL_W,
            RET_W, RET_W, RET_W, RET_W,
            MLA_Q_RANK, MLA_KV_RANK, MLA_ROPE)
IN_WIDTH = sum(IN_SIZES)
IN_OFFSETS = tuple(int(v) for v in np.cumsum(IN_SIZES)[:-1])

kernel_name = 'hybrid_gated_fox_dilated_retention_mla_block'


def rms_norm(x, g):
    xf = x.astype(jnp.float32)
    y = xf * lax.rsqrt(jnp.mean(xf * xf, axis=-1, keepdims=True) + RMS_EPS)
    return (y * g.astype(jnp.float32)).astype(x.dtype)


def heads(t, n):
    B, S, _ = t.shape
    return t.reshape(B, S, n, -1).transpose(0, 2, 1, 3)


def merge_heads(t):
    B, H, S, d = t.shape
    return t.transpose(0, 2, 1, 3).reshape(B, S, H * d)


def rope(x, pos):
    half = x.shape[-1] // 2
    inv = ROPE_THETA ** (-jnp.arange(half, dtype=jnp.float32) / half)
    ang = pos.astype(jnp.float32)[:, None] * inv[None, :]
    cos, sin = jnp.cos(ang).astype(x.dtype), jnp.sin(ang).astype(x.dtype)
    x1, x2 = x[..., :half], x[..., half:]
    return jnp.concatenate([x1 * cos - x2 * sin, x1 * sin + x2 * cos], axis=-1)


def causal_block_attention(q, k, v, scale, decay=None):
    B, H, S, dk = q.shape
    nb = S // Q_BLOCK
    kpos = jnp.arange(S)
    qb = jnp.moveaxis(q.reshape(B, H, nb, Q_BLOCK, dk), 2, 0)
    idx = jnp.arange(nb)
    if decay is None:
        xs = (idx, qb)
    else:
        xs = (idx, qb, jnp.moveaxis(decay.reshape(B, H, nb, Q_BLOCK), 2, 0))

    def block(args):
        i, qi = args[0], args[1]
        s = jnp.einsum('bhqd,bhkd->bhqk', qi, k).astype(jnp.float32) * scale
        if decay is not None:
            s = s + args[2][..., :, None] - decay[..., None, :]
        qpos = i * Q_BLOCK + jnp.arange(Q_BLOCK)
        s = jnp.where(kpos[None, :] <= qpos[:, None], s, -jnp.inf)
        p = jax.nn.softmax(s, axis=-1).astype(v.dtype)
        return jnp.einsum('bhqk,bhkd->bhqd', p, v)

    o = lax.map(block, xs)
    return jnp.moveaxis(o, 0, 2).reshape(B, H, S, v.shape[-1])


def banded_window_attention(q, k, v, n, scale):
    *lead, L, d = q.shape
    nb = -(-L // n)
    pad_cfg = [(0, 0)] * len(lead) + [(0, nb * n - L), (0, 0)]

    def blocks(t):
        return jnp.pad(t, pad_cfg).reshape(*lead, nb, n, t.shape[-1])

    def with_prev(t):
        prev = jnp.concatenate([jnp.zeros_like(t[..., :1, :, :]), t[..., :-1, :, :]], axis=-3)
        return jnp.concatenate([prev, t], axis=-2)

    qb = blocks(q)
    kk, vv = with_prev(blocks(k)), with_prev(blocks(v))
    s = jnp.einsum('...qd,...kd->...qk', qb, kk).astype(jnp.float32) * scale
    qi = jnp.arange(n)
    ki = jnp.arange(2 * n) - n
    dist = qi[:, None] - ki[None, :]
    kabs = jnp.arange(nb)[:, None, None] * n + ki[None, None, :]
    valid = (dist >= 0) & (dist <= n) & (kabs >= 0)
    s = jnp.where(valid, s, -jnp.inf)
    lse = jax.nn.logsumexp(s, axis=-1)
    p = jnp.exp(s - lse[..., None]).astype(v.dtype)
    o = jnp.einsum('...qk,...kd->...qd', p, vv)
    o = o.reshape(*lead, nb * n, v.shape[-1])[..., :L, :]
    lse = lse.reshape(*lead, nb * n)[..., :L]
    return o, lse


def dilated_attention(q, k, v):
    B, H, S, d = q.shape
    scale = d ** -0.5
    outs, lses = [], []
    for window, dil in DIL_PATTERNS:
        L = S // dil

        def to_res(t):
            return t.reshape(B, H, L, dil, t.shape[-1]).transpose(0, 1, 3, 2, 4)

        o, lse = banded_window_attention(to_res(q), to_res(k), to_res(v), window // dil, scale)
        outs.append(o.transpose(0, 1, 3, 2, 4).reshape(B, H, S, d))
        lses.append(lse.transpose(0, 1, 3, 2).reshape(B, H, S))
    w = jax.nn.softmax(jnp.stack(lses), axis=0)
    return jnp.einsum('pbhs,pbhsd->bhsd', w.astype(q.dtype), jnp.stack(outs))


def retention(q, k, v):
    B, H, S, dk = q.shape
    dv = v.shape[-1]
    C = RET_CHUNK
    nc = S // C
    log_g = jnp.log1p(-(2.0 ** (-5.0 - jnp.arange(H, dtype=jnp.float32))))
    pos = jnp.arange(C, dtype=jnp.float32)
    rel = pos[:, None] - pos[None, :]
    d_in = jnp.where(rel >= 0, jnp.exp(jnp.maximum(rel, 0.0)[None] * log_g[:, None, None]), 0.0)
    zeta = jnp.exp((C - 1 - pos)[None, :] * log_g[:, None])
    xi = jnp.exp((pos + 1)[None, :] * log_g[:, None])
    g_chunk = jnp.exp(C * log_g)
    qc = q.astype(jnp.float32).reshape(B, H, nc, C, dk)
    kc = k.astype(jnp.float32).reshape(B, H, nc, C, dk)
    vc = v.astype(jnp.float32).reshape(B, H, nc, C, dv)
    scores = jnp.einsum('bhcnd,bhcmd->bhcnm', qc, kc) * d_in[None, :, None]
    inner = jnp.einsum('bhcnm,bhcme->bhcne', scores, vc)
    kv = jnp.einsum('bhcmd,bhcme->bhcde', kc * zeta[None, :, None, :, None], vc)

    def step(R, xs):
        q_i, kv_i = xs
        out = jnp.einsum('bhnd,bhde->bhne', q_i, R)
        return g_chunk[None, :, None, None] * R + kv_i, out

    R0 = jnp.zeros((B, H, dk, dv), jnp.float32)
    _, cross = lax.scan(step, R0, (jnp.moveaxis(qc, 2, 0), jnp.moveaxis(kv, 2, 0)))
    cross = jnp.moveaxis(cross, 0, 2) * xi[None, :, None, :, None]
    return (inner + cross).reshape(B, H, S, dv)


def hybrid_mixer(h, w_in, b_forget, ret_gn_gain, mla_q_norm, mla_kv_norm, w_uq, w_ukv, w_gate, w_branch, w_out):
    B, S, _ = h.shape
    pos = jnp.arange(S)
    (fq, fk, fv, ff, dq, dk, dv, rq, rk, rv, rg, cq, ckv, kr) = jnp.split(h @ w_in, IN_OFFSETS, axis=-1)

    log_f = jax.nn.log_sigmoid((ff + b_forget).astype(jnp.float32))
    decay = jnp.cumsum(log_f, axis=1).transpose(0, 2, 1)
    o_a = causal_block_attention(heads(fq, FOX_HEADS), heads(fk, FOX_HEADS), heads(fv, FOX_HEADS),
                                 HEAD_DIM ** -0.5, decay)

    o_b = dilated_attention(heads(dq, DIL_HEADS), heads(dk, DIL_HEADS), heads(dv, DIL_HEADS))

    o_c = retention(rope(heads(rq, RET_HEADS), pos), rope(heads(rk, RET_HEADS), pos) * HEAD_DIM ** -0.5,
                    heads(rv, RET_HEADS))
    mu = jnp.mean(o_c, axis=-1, keepdims=True)
    var = jnp.mean(jnp.square(o_c - mu), axis=-1, keepdims=True)
    o_c = merge_heads((o_c - mu) * lax.rsqrt(var + GN_EPS)).astype(h.dtype) * ret_gn_gain
    o_c = jax.nn.silu(rg) * o_c

    q = heads(rms_norm(cq, mla_q_norm) @ w_uq, MLA_HEADS)
    kv = heads(rms_norm(ckv, mla_kv_norm) @ w_ukv, MLA_HEADS)
    k_rope = jnp.broadcast_to(rope(kr[:, None], pos), (B, MLA_HEADS, S, MLA_ROPE))
    q_mla = jnp.concatenate([q[..., :MLA_NOPE], rope(q[..., MLA_NOPE:], pos)], axis=-1)
    k_mla = jnp.concatenate([kv[..., :MLA_NOPE], k_rope], axis=-1)
    o_d = causal_block_attention(q_mla, k_mla, kv[..., MLA_NOPE:], (MLA_NOPE + MLA_ROPE) ** -0.5)

    branches = jnp.stack([merge_heads(o_a), merge_heads(o_b), o_c, merge_heads(o_d)], axis=2)
    gates = jax.nn.sigmoid(h @ w_gate).reshape(B, S, N_BRANCH, D_MODEL)
    merged = jnp.sum(jnp.einsum('bsnw,nwd->bsnd', branches, w_branch) * gates, axis=2)
    return merged @ w_out


def setup_inputs(seed: int = 0) -> dict:
    key = jax.random.key(seed)
    ks = jax.random.split(key, 20)

    def nrm(k, shape, fan_in):
        return jax.random.normal(k, shape, jnp.float32) * fan_in ** -0.5

    def gain(k, shape):
        return 1.0 + 0.05 * jax.random.normal(k, shape, jnp.float32)

    L = DEPTH
    return {
        'x': jax.random.normal(ks[0], (BATCH, SEQ, D_MODEL), jnp.float32),
        'w_in': nrm(ks[1], (L, D_MODEL, IN_WIDTH), D_MODEL),
        'b_forget': 0.1 * jax.random.normal(ks[2], (L, FOX_HEADS), jnp.float32),
        'ret_gn_gain': gain(ks[3], (L, RET_W)),
        'mla_q_norm': gain(ks[4], (L, MLA_Q_RANK)),
        'mla_kv_norm': gain(ks[5], (L, MLA_KV_RANK)),
        'w_uq': nrm(ks[6], (L, MLA_Q_RANK, MLA_HEADS * (MLA_NOPE + MLA_ROPE)), MLA_Q_RANK),
        'w_ukv': nrm(ks[7], (L, MLA_KV_RANK, MLA_HEADS * (MLA_NOPE + MLA_V)), MLA_KV_RANK),
        'w_gate': nrm(ks[8], (L, D_MODEL, N_BRANCH * D_MODEL), D_MODEL),
        'w_branch': nrm(ks[9], (L, N_BRANCH, BRANCH_WIDTH, D_MODEL), BRANCH_WIDTH),
        'w_out': nrm(ks[10], (L, D_MODEL, D_MODEL), D_MODEL),
        'g_pre_mix': gain(ks[11], (L, D_MODEL)),
        'g_post_mix': gain(ks[12], (L, D_MODEL)),
        'g_pre_ffn': gain(ks[13], (L, D_MODEL)),
        'g_post_ffn': gain(ks[14], (L, D_MODEL)),
        'w_ffn_gate': nrm(ks[15], (L, D_MODEL, D_FF), D_MODEL),
        'w_ffn_up': nrm(ks[16], (L, D_MODEL, D_FF), D_MODEL),
        'w_ffn_down': nrm(ks[17], (L, D_FF, D_MODEL), D_FF),
    }


def reference(x, w_in, b_forget, ret_gn_gain, mla_q_norm, mla_kv_norm, w_uq, w_ukv, w_gate, w_branch, w_out,
              g_pre_mix, g_post_mix, g_pre_ffn, g_post_ffn, w_ffn_gate, w_ffn_up, w_ffn_down):
    for l in range(DEPTH):
        h = rms_norm(x, g_pre_mix[l])
        mix = hybrid_mixer(h, w_in[l], b_forget[l], ret_gn_gain[l], mla_q_norm[l], mla_kv_norm[l],
                           w_uq[l], w_ukv[l], w_gate[l], w_branch[l], w_out[l])
        x = x + rms_norm(mix, g_post_mix[l])
        h = rms_norm(x, g_pre_ffn[l])
        f = (jax.nn.silu(h @ w_ffn_gate[l]) * (h @ w_ffn_up[l])) @ w_ffn_down[l]
        x = x + rms_norm(f, g_post_ffn[l])
    return x
```

```python
import functools
import math

import jax
import jax.numpy as jnp
import numpy as np
from jax import lax
from jax.experimental import pallas as pl
from jax.experimental.pallas import tpu as pltpu

HEAD_DIM = 64
N_HEADS = 4
PAIR_W = N_HEADS * HEAD_DIM
LANES = 128
PAD_W = N_HEADS * LANES
DIL_PATTERNS = ((128, 1), (512, 4), (2048, 16))
DIL_BLOCK = 128
RET_CHUNK = 128
MLA_Q_RANK = 256
MLA_KV_RANK = 128
MLA_NOPE = 64
MLA_ROPE = 32
MLA_V = 64
ROPE_THETA = 10000.0
RMS_EPS = 1e-6
GN_EPS = 1e-5
NEG_BIG = -1e30

MXU_DTYPE = jnp.bfloat16
ROW_TILE = 512
VMEM_LIMIT = 56 * 1024 * 1024

_C_FQ, _C_FK, _C_FV, _C_FF = 0, 256, 512, 1024
_C_DQ, _C_DK, _C_DV = 1152, 1408, 1664
_C_RQ, _C_RK, _C_RV, _C_RG = 2176, 2432, 2688, 2944
_C_CQ, _C_CKV, _C_KR = 3200, 3456, 3584
IN_CAT_W = 3712


def _f32(x):
    return x.astype(jnp.float32)


def _rms(x, g):
    ms = jnp.mean(x * x, axis=-1, keepdims=True)
    return x * lax.rsqrt(ms + RMS_EPS) * g


def _mm(a, b):
    return jnp.dot(a, b, preferred_element_type=jnp.float32)


def _mm_nt(a, b):
    return lax.dot_general(a, b, (((1,), (1,)), ((), ())), preferred_element_type=jnp.float32)


def _mm_tn(a, b):
    return lax.dot_general(a, b, (((0,), (0,)), ((), ())), preferred_element_type=jnp.float32)


def _lane(shape):
    return lax.broadcasted_iota(jnp.int32, shape, len(shape) - 1)


def _rope_chunks(a, cos, sin_a, sin_b, half):
    outs = []
    for c in range(a.shape[1] // LANES):
        t = a[:, c * LANES:(c + 1) * LANES]
        up = pltpu.roll(t, LANES - half, axis=1)
        dn = pltpu.roll(t, half, axis=1)
        outs.append(t * cos + up * sin_a + dn * sin_b)
    return outs[0] if len(outs) == 1 else jnp.concatenate(outs, axis=1)


def _inproj_kernel(x_ref, g_ref, w_ref, bf_ref, rcos_ref, rsa_ref, rsb_ref, mcos_ref, msa_ref, msb_ref,
                   qn_ref, kvn_ref, wuq_ref, wuk_ref, wuv_ref,
                   fq_ref, fk_ref, fva_ref, c_ref, ct_ref,
                   dq_ref, dk_ref, dva_ref,
                   rq_ref, rk_ref, rv_ref, sg_ref,
                   mq_ref, mk_ref, mva_ref,
                   carry_ref, *, mla_scale):
    tm = x_ref.shape[0]
    h = _rms(x_ref[...], g_ref[...]).astype(MXU_DTYPE)

    def proj(lo, width):
        return _mm(h, w_ref[:, lo:lo + width])

    ones_up = jnp.where(_lane((1, PAD_W)) % LANES >= HEAD_DIM, 1.0, 0.0)

    fq_ref[...] = (proj(_C_FQ, PAIR_W) * HEAD_DIM ** -0.5).astype(fq_ref.dtype)
    fk_ref[...] = proj(_C_FK, PAIR_W).astype(fk_ref.dtype)
    fva_ref[...] = (proj(_C_FV, PAD_W) + ones_up).astype(fva_ref.dtype)
    ff = proj(_C_FF, LANES) + bf_ref[...]
    log_f = jnp.minimum(ff, 0.0) - jnp.log1p(jnp.exp(-jnp.abs(ff)))
    rows = lax.broadcasted_iota(jnp.int32, (tm, LANES), 0)
    cs = log_f
    step = 1
    while step < tm:
        cs = cs + jnp.where(rows >= step, pltpu.roll(cs, step, axis=0), 0.0)
        step *= 2

    @pl.when(pl.program_id(1) == 0)
    def _():
        carry_ref[...] = jnp.zeros_like(carry_ref)

    cs = cs + carry_ref[...]
    carry_ref[...] = cs[tm - 1:tm, :]
    c_ref[...] = cs
    ct_ref[...] = jnp.transpose(cs)[0:8, :]

    dq_ref[...] = (proj(_C_DQ, PAIR_W) * HEAD_DIM ** -0.5).astype(dq_ref.dtype)
    dk_ref[...] = proj(_C_DK, PAIR_W).astype(dk_ref.dtype)
    dva_ref[...] = (proj(_C_DV, PAD_W) + ones_up).astype(dva_ref.dtype)

    rcos, rsa, rsb = rcos_ref[...], rsa_ref[...], rsb_ref[...]
    rq_ref[...] = _rope_chunks(proj(_C_RQ, PAIR_W), rcos, rsa, rsb, HEAD_DIM // 2).astype(rq_ref.dtype)
    rk_ref[...] = _rope_chunks(proj(_C_RK, PAIR_W), rcos, rsa, rsb, HEAD_DIM // 2) * HEAD_DIM ** -0.5
    rv_ref[...] = proj(_C_RV, PAIR_W).astype(rv_ref.dtype)
    rg = proj(_C_RG, PAIR_W)
    sg_ref[...] = rg / (1.0 + jnp.exp(-rg))

    mcos, msa, msb = mcos_ref[...], msa_ref[...], msb_ref[...]
    cq = _rms(proj(_C_CQ, MLA_Q_RANK), qn_ref[...]).astype(MXU_DTYPE)
    q = _rope_chunks(_mm(cq, wuq_ref[...]), mcos, msa, msb, MLA_ROPE // 2)
    mq_ref[...] = (q * mla_scale).astype(mq_ref.dtype)
    ckv = _rms(proj(_C_CKV, MLA_KV_RANK), kvn_ref[...]).astype(MXU_DTYPE)
    kr = _rope_chunks(proj(_C_KR, LANES), mcos, msa, msb, MLA_ROPE // 2)
    mk_ref[...] = (_mm(ckv, wuk_ref[...]) + jnp.concatenate([kr] * N_HEADS, axis=1)).astype(mk_ref.dtype)
    mva_ref[...] = (_mm(ckv, wuv_ref[...]) + ones_up).astype(mva_ref.dtype)


def _inproj(x, g, w_cat, bf, ret_tabs, mla_tabs, qn, kvn, wuq, wuk, wuv):
    B, S, D = x.shape
    tm = ROW_TILE
    ns = S // tm
    row = lambda w: pl.BlockSpec((None, tm, w), lambda b, i: (b, i, 0))
    full = lambda a: pl.BlockSpec(a.shape, lambda b, i: (0,) * a.ndim)
    tab = pl.BlockSpec((tm, LANES), lambda b, i: (i, 0))
    bf16, f32 = MXU_DTYPE, jnp.float32
    out_defs = [
        (PAIR_W, bf16), (PAIR_W, bf16), (PAD_W, bf16), (LANES, f32), None,
        (PAIR_W, bf16), (PAIR_W, bf16), (PAD_W, bf16),
        (PAIR_W, bf16), (PAIR_W, f32), (PAIR_W, bf16), (PAIR_W, f32),
        (PAD_W, bf16), (PAD_W, bf16), (PAD_W, bf16),
    ]
    out_shape, out_specs = [], []
    for d in out_defs:
        if d is None:
            out_shape.append(jax.ShapeDtypeStruct((B, ns, 8, tm), f32))
            out_specs.append(pl.BlockSpec((None, None, 8, tm), lambda b, i: (b, i, 0, 0)))
        else:
            out_shape.append(jax.ShapeDtypeStruct((B, S, d[0]), d[1]))
            out_specs.append(row(d[0]))
    consts = [g, w_cat, bf]
    tabs = list(ret_tabs) + list(mla_tabs)
    tail = [qn, kvn, wuq, wuk, wuv]
    return pl.pallas_call(
        functools.partial(_inproj_kernel, mla_scale=(MLA_NOPE + MLA_ROPE) ** -0.5),
        grid=(B, ns),
        in_specs=[row(D)] + [full(a) for a in consts] + [tab] * 6 + [full(a) for a in tail],
        out_specs=out_specs,
        out_shape=out_shape,
        scratch_shapes=[pltpu.VMEM((1, LANES), f32)],
        compiler_params=pltpu.CompilerParams(
            dimension_semantics=("arbitrary", "arbitrary"), vmem_limit_bytes=VMEM_LIMIT),
        name="inproj",
    )(x, *consts, *tabs, *tail)


def _head_q(q_ref, h, pair_q):
    if pair_q:
        g, e = divmod(h, 2)
        q2 = q_ref[:, g * LANES:(g + 1) * LANES]
        keep = (_lane(q2.shape) >= HEAD_DIM) if e else (_lane(q2.shape) < HEAD_DIM)
        return jnp.where(keep, q2, jnp.zeros_like(q2)), g
    return q_ref[:, h * LANES:(h + 1) * LANES], h


def _pack_pair(o_even, o_odd):
    return jnp.where(_lane(o_even.shape) < HEAD_DIM, o_even, pltpu.roll(o_odd, HEAD_DIM, axis=1))


def _flash_kernel(*refs, pair_q, use_decay, tk):
    if use_decay:
        q_ref, k_ref, va_ref, c_ref, ct_ref, o_ref = refs
    else:
        q_ref, k_ref, va_ref, o_ref = refs
    tq = q_ref.shape[0]
    qi = pl.program_id(1)
    n_sub = tq // tk
    outs = []
    for h in range(N_HEADS):
        qh, kg = _head_q(q_ref, h, pair_q)
        kcols = slice(kg * LANES, (kg + 1) * LANES)
        vcols = slice(h * LANES, (h + 1) * LANES)
        ci = c_ref[:, h:h + 1] if use_decay else None

        def scores(j, row0):
            start = pl.multiple_of(j * tk, tk)
            s = _mm_nt(qh[row0:, :], k_ref[pl.ds(start, tk), kcols])
            if use_decay:
                s = s + ci[row0:, :] - ct_ref[j, h:h + 1, :]
            return s, start

        def update(carry, s, start, row0):
            m, acc = carry
            m_new = jnp.maximum(m, jnp.max(s, axis=-1, keepdims=True))
            p = jnp.exp(s - m_new).astype(MXU_DTYPE)
            return m_new, jnp.exp(m - m_new) * acc + _mm(p, va_ref[pl.ds(start, tk), vcols])

        def body(j, carry):
            s, start = scores(j, 0)
            return update(carry, s, start, 0)

        carry = (jnp.full((tq, 1), NEG_BIG, jnp.float32), jnp.zeros((tq, LANES), jnp.float32))
        carry = lax.fori_loop(0, qi * n_sub, body, carry)
        m, acc = carry
        for d in range(n_sub):
            row0 = d * tk
            s, start = scores(qi * n_sub + d, row0)
            r = lax.broadcasted_iota(jnp.int32, s.shape, 0)
            c = lax.broadcasted_iota(jnp.int32, s.shape, 1)
            s = jnp.where(c <= r, s, NEG_BIG)
            m_d, acc_d = update((m[row0:, :], acc[row0:, :]), s, start, row0)
            if row0:
                m = jnp.concatenate([m[:row0, :], m_d], axis=0)
                acc = jnp.concatenate([acc[:row0, :], acc_d], axis=0)
            else:
                m, acc = m_d, acc_d
        outs.append(acc / pltpu.roll(acc, HEAD_DIM, axis=1))
    for g in range(N_HEADS // 2):
        o_ref[:, g * LANES:(g + 1) * LANES] = _pack_pair(outs[2 * g], outs[2 * g + 1]).astype(o_ref.dtype)


def _flash(q, k, va, c=None, ct=None):
    B, S, qw = q.shape
    tq = tk = ROW_TILE
    pair_q = qw == PAIR_W
    use_decay = c is not None
    seq = lambda w: pl.BlockSpec((None, S, w), lambda b, i: (b, 0, 0))
    in_specs = [pl.BlockSpec((None, tq, qw), lambda b, i: (b, i, 0)), seq(k.shape[-1]), seq(PAD_W)]
    args = [q, k, va]
    if use_decay:
        in_specs += [pl.BlockSpec((None, tq, LANES), lambda b, i: (b, i, 0)),
                     pl.BlockSpec((None,) + ct.shape[1:], lambda b, i: (b, 0, 0, 0))]
        args += [c, ct]
    return pl.pallas_call(
        functools.partial(_flash_kernel, pair_q=pair_q, use_decay=use_decay, tk=tk),
        grid=(B, S // tq),
        in_specs=in_specs,
        out_specs=pl.BlockSpec((None, tq, PAIR_W), lambda b, i: (b, i, 0)),
        out_shape=jax.ShapeDtypeStruct((B, S, PAIR_W), MXU_DTYPE),
        compiler_params=pltpu.CompilerParams(
            dimension_semantics=("arbitrary", "arbitrary"), vmem_limit_bytes=VMEM_LIMIT),
        name="fox_attn" if use_decay else "mla_attn",
    )(*args)


def _dilated_kernel(q_ref, kc_ref, kp_ref, vc_ref, vp_ref, o_ref):
    tl = q_ref.shape[0]
    n = DIL_BLOCK
    first_col = jnp.where(pl.program_id(2) == 0, n, 0)
    r = lax.broadcasted_iota(jnp.int32, (n, 2 * n), 0)
    c = lax.broadcasted_iota(jnp.int32, (n, 2 * n), 1)
    band = (c >= r) & (c <= r + n)
    band_first = band & (c >= first_col)
    for a in range(tl // n):
        rows = slice(a * n, (a + 1) * n)
        for h in range(N_HEADS):
            qh, kg = _head_q(q_ref.at[rows, :], h, True)
            kcols = slice(kg * LANES, (kg + 1) * LANES)
            vcols = slice(h * LANES, (h + 1) * LANES)
            if a == 0:
                kwin = jnp.concatenate([kp_ref[:, kcols], kc_ref[0:n, kcols]], axis=0)
                vwin = jnp.concatenate([vp_ref[:, vcols], vc_ref[0:n, vcols]], axis=0)
                valid = band_first
            else:
                kwin = kc_ref[(a - 1) * n:(a + 1) * n, kcols]
                vwin = vc_ref[(a - 1) * n:(a + 1) * n, vcols]
                valid = band
            s = jnp.where(valid, _mm_nt(qh, kwin), NEG_BIG)
            m = jnp.max(s, axis=-1, keepdims=True)
            acc = _mm(jnp.exp(s - m).astype(MXU_DTYPE), vwin)
            o_ref[rows, vcols] = jnp.where(_lane(acc.shape) < HEAD_DIM + HEAD_DIM // 2, acc, m)


def _dilated(q, k, va, dil):
    B, S, _ = q.shape
    L = S // dil
    tl = min(L, ROW_TILE)
    per = tl // DIL_BLOCK
    q2 = q.reshape(B, L, dil * PAIR_W)
    k2 = k.reshape(B, L, dil * PAIR_W)
    v2 = va.reshape(B, L, dil * PAD_W)
    cur = lambda w: pl.BlockSpec((None, tl, w), lambda b, p, i: (b, i, p))
    prev = lambda w: pl.BlockSpec((None, DIL_BLOCK, w), lambda b, p, i: (b, jnp.maximum(i * per - 1, 0), p))
    out = pl.pallas_call(
        _dilated_kernel,
        grid=(B, dil, L // tl),
        in_specs=[cur(PAIR_W), cur(PAIR_W), prev(PAIR_W), cur(PAD_W), prev(PAD_W)],
        out_specs=cur(PAD_W),
        out_shape=jax.ShapeDtypeStruct((B, L, dil * PAD_W), jnp.float32),
        compiler_params=pltpu.CompilerParams(
            dimension_semantics=("arbitrary", "arbitrary", "arbitrary"), vmem_limit_bytes=VMEM_LIMIT),
        name="dilated_r%d" % dil,
    )(q2, k2, k2, v2, v2)
    return out.reshape(B, S, PAD_W)


def _retention_kernel(q_ref, k_ref, v_ref, sg_ref, gain_ref, o_ref, state_ref):
    C = RET_CHUNK
    T = q_ref.shape[0]

    @pl.when(pl.program_id(1) == 0)
    def _():
        state_ref[...] = jnp.zeros_like(state_ref)

    lane = _lane((C, LANES))
    low = lane < HEAD_DIM
    pos = _f32(lax.broadcasted_iota(jnp.int32, (C, LANES), 0))
    rel = pos - _f32(lane)
    row_low = lax.broadcasted_iota(jnp.int32, (LANES, LANES), 0) < HEAD_DIM
    same_head = row_low == (_lane((LANES, LANES)) < HEAD_DIM)
    for g in range(N_HEADS // 2):
        lg = [math.log1p(-(2.0 ** (-5.0 - (2 * g + e)))) for e in range(2)]
        lg_lane = jnp.where(low, lg[0], lg[1])
        xi = jnp.exp((pos + 1.0) * lg_lane)
        zeta = jnp.exp((C - 1.0 - pos) * lg_lane)
        g_rows = jnp.where(row_low, math.exp(C * lg[0]), math.exp(C * lg[1]))
        d_in = [jnp.where(rel >= 0, jnp.exp(jnp.maximum(rel, 0.0) * lg[e]), 0.0) for e in range(2)]
        cols = slice(g * LANES, (g + 1) * LANES)
        R = state_ref[g]
        for ch in range(T // C):
            rows = slice(ch * C, (ch + 1) * C)
            q2 = q_ref[rows, cols]
            kf = k_ref[rows, cols]
            k2 = kf.astype(MXU_DTYPE)
            v2 = v_ref[rows, cols]
            inner = []
            for e in range(2):
                qm = jnp.where(low if e == 0 else ~low, q2, jnp.zeros_like(q2))
                sc = _mm_nt(qm, k2) * d_in[e]
                inner.append(_mm(sc.astype(MXU_DTYPE), v2))
            cross = _mm(q2, R.astype(MXU_DTYPE)) * xi
            out = jnp.where(low, inner[0], inner[1]) + cross
            kv = _mm_tn((kf * zeta).astype(MXU_DTYPE), v2)
            R = g_rows * R + jnp.where(same_head, kv, 0.0)
            mu = jnp.where(low,
                           jnp.sum(jnp.where(low, out, 0.0), axis=-1, keepdims=True),
                           jnp.sum(jnp.where(low, 0.0, out), axis=-1, keepdims=True)) * (1.0 / HEAD_DIM)
            dlt = out - mu
            sq = dlt * dlt
            var = jnp.where(low,
                            jnp.sum(jnp.where(low, sq, 0.0), axis=-1, keepdims=True),
                            jnp.sum(jnp.where(low, 0.0, sq), axis=-1, keepdims=True)) * (1.0 / HEAD_DIM)
            y = dlt * lax.rsqrt(var + GN_EPS) * gain_ref[:, cols]
            o_ref[rows, cols] = (sg_ref[rows, cols] * y).astype(o_ref.dtype)
        state_ref[g] = R


def _retention(rq, rk, rv, sg, gain):
    B, S, _ = rq.shape
    T = ROW_TILE
    row = pl.BlockSpec((None, T, PAIR_W), lambda b, i: (b, i, 0))
    return pl.pallas_call(
        _retention_kernel,
        grid=(B, S // T),
        in_specs=[row, row, row, row, pl.BlockSpec((1, PAIR_W), lambda b, i: (0, 0))],
        out_specs=row,
        out_shape=jax.ShapeDtypeStruct((B, S, PAIR_W), MXU_DTYPE),
        scratch_shapes=[pltpu.VMEM((N_HEADS // 2, LANES, LANES), jnp.float32)],
        compiler_params=pltpu.CompilerParams(
            dimension_semantics=("arbitrary", "arbitrary"), vmem_limit_bytes=VMEM_LIMIT),
        name="retention",
    )(rq, rk, rv, sg, gain)


def _merge_kernel(x_ref, gpre_ref, oa_ref, d1_ref, d2_ref, d3_ref, oc_ref, od_ref,
                  wg_ref, wba_ref, wbb_ref, wbc_ref, wbd_ref, wo_ref, gpost_ref, y_ref):
    D = x_ref.shape[1]
    x = x_ref[...]
    h = _rms(x, gpre_ref[...]).astype(MXU_DTYPE)

    ob = []
    for hd in range(N_HEADS):
        cols = slice(hd * LANES, (hd + 1) * LANES)
        accs = [r[:, cols] for r in (d1_ref, d2_ref, d3_ref)]
        ms = [a[:, LANES - 1:LANES] for a in accs]
        m = jnp.maximum(jnp.maximum(ms[0], ms[1]), ms[2])
        comb = sum(jnp.exp(mp - m) * a for mp, a in zip(ms, accs))
        o = comb / comb[:, HEAD_DIM:HEAD_DIM + 1]
        ob.append(jnp.where(_lane(o.shape) < HEAD_DIM, o, 0.0).astype(MXU_DTYPE))
    ob = jnp.concatenate(ob, axis=1)

    branches = ((oa_ref[...], wba_ref), (ob, wbb_ref), (oc_ref[...], wbc_ref), (od_ref[...], wbd_ref))
    merged = None
    for n, (o, wb_ref) in enumerate(branches):
        gate = 1.0 / (1.0 + jnp.exp(-_mm(h, wg_ref[:, n * D:(n + 1) * D])))
        term = _mm(o, wb_ref[...]) * gate
        merged = term if merged is None else merged + term
    mix = _mm(merged.astype(MXU_DTYPE), wo_ref[...])
    y_ref[...] = x + _rms(mix, gpost_ref[...])


def _merge(x2, gpre, oa, d1, d2, d3, oc, od, wg, wba, wbb, wbc, wbd, wo, gpost):
    N, D = x2.shape
    tm = ROW_TILE
    row = lambda w: pl.BlockSpec((tm, w), lambda i: (i, 0))
    full = lambda a: pl.BlockSpec(a.shape, lambda i: (0,) * a.ndim)
    return pl.pallas_call(
        _merge_kernel,
        grid=(N // tm,),
        in_specs=[row(D), full(gpre), row(PAIR_W), row(PAD_W), row(PAD_W), row(PAD_W), row(PAIR_W), row(PAIR_W),
                  full(wg), full(wba), full(wbb), full(wbc), full(wbd), full(wo), full(gpost)],
        out_specs=row(D),
        out_shape=jax.ShapeDtypeStruct((N, D), jnp.float32),
        compiler_params=pltpu.CompilerParams(dimension_semantics=("arbitrary",), vmem_limit_bytes=VMEM_LIMIT),
        name="merge",
    )(x2, gpre, oa, d1, d2, d3, oc, od, wg, wba, wbb, wbc, wbd, wo, gpost)


def _ffn_kernel(x_ref, gpre_ref, wgate_ref, wup_ref, wdown_ref, gpost_ref, y_ref):
    x = x_ref[...]
    h = _rms(x, gpre_ref[...]).astype(MXU_DTYPE)
    gate = _mm(h, wgate_ref[...])
    act = (gate / (1.0 + jnp.exp(-gate)) * _mm(h, wup_ref[...])).astype(MXU_DTYPE)
    f = _mm(act, wdown_ref[...])
    y_ref[...] = x + _rms(f, gpost_ref[...])


def _ffn(x2, gpre, wgate, wup, wdown, gpost):
    N, D = x2.shape
    tm = ROW_TILE
    row = pl.BlockSpec((tm, D), lambda i: (i, 0))
    full = lambda a: pl.BlockSpec(a.shape, lambda i: (0,) * a.ndim, pipeline_mode=pl.Buffered(1))
    return pl.pallas_call(
        _ffn_kernel,
        grid=(N // tm,),
        in_specs=[row, full(gpre), full(wgate), full(wup), full(wdown), full(gpost)],
        out_specs=row,
        out_shape=jax.ShapeDtypeStruct((N, D), jnp.float32),
        compiler_params=pltpu.CompilerParams(dimension_semantics=("arbitrary",), vmem_limit_bytes=VMEM_LIMIT),
        name="ffn",
    )(x2, gpre, wgate, wup, wdown, gpost)


def _rope_tables(S, half, lo):
    period = 2 * half if lo == 0 else LANES
    inv = ROPE_THETA ** (-jnp.arange(half, dtype=jnp.float32) / half)
    ang = jnp.arange(S, dtype=jnp.float32)[:, None] * inv[None, :]
    cos, sin = jnp.cos(ang), jnp.sin(ang)
    reps = LANES // period
    pad_lo = jnp.zeros((S, lo), jnp.float32)
    pad_hi = jnp.zeros((S, period - lo - 2 * half), jnp.float32)
    zeros = jnp.zeros((S, half), jnp.float32)
    cos_t = jnp.concatenate([pad_lo + 1.0, cos, cos, pad_hi + 1.0], axis=1)
    sin_up = jnp.concatenate([pad_lo, -sin, zeros, pad_hi], axis=1)
    sin_dn = jnp.concatenate([pad_lo, zeros, sin, pad_hi], axis=1)
    return tuple(jnp.tile(t, (1, reps)) for t in (cos_t, sin_up, sin_dn))


def _pad_heads(w, width=HEAD_DIM):
    K = w.shape[0]
    w = w.reshape(K, N_HEADS, width)
    return jnp.pad(w, ((0, 0), (0, 0), (0, LANES - width))).reshape(K, PAD_W)


def _layer_weights(w_in, b_forget, w_uq, w_ukv, w_branch):
    sizes = (PAIR_W, PAIR_W, PAIR_W, N_HEADS, PAIR_W, PAIR_W, PAIR_W, PAIR_W, PAIR_W, PAIR_W, PAIR_W,
             MLA_Q_RANK, MLA_KV_RANK, MLA_ROPE)
    offs = np.cumsum(sizes)[:-1]
    fq, fk, fv, ff, dq, dk, dv, rq, rk, rv, rg, cq, ckv, kr = jnp.split(w_in, offs, axis=1)
    D = w_in.shape[0]
    ff_p = jnp.pad(ff, ((0, 0), (0, LANES - N_HEADS)))
    kr_p = jnp.pad(kr, ((0, 0), (MLA_NOPE, LANES - MLA_NOPE - MLA_ROPE)))
    w_cat = jnp.concatenate([fq, fk, _pad_heads(fv), ff_p, dq, dk, _pad_heads(dv), rq, rk, rv, rg, cq, ckv, kr_p],
                            axis=1).astype(MXU_DTYPE)
    assert w_cat.shape == (D, IN_CAT_W)
    bf = jnp.pad(b_forget, (0, LANES - N_HEADS)).reshape(1, LANES)
    wuq = _pad_heads(w_uq, MLA_NOPE + MLA_ROPE).astype(MXU_DTYPE)
    ukv = w_ukv.reshape(MLA_KV_RANK, N_HEADS, MLA_NOPE + MLA_V)
    wuk = _pad_heads(ukv[:, :, :MLA_NOPE].reshape(MLA_KV_RANK, -1)).astype(MXU_DTYPE)
    wuv = _pad_heads(ukv[:, :, MLA_NOPE:].reshape(MLA_KV_RANK, -1)).astype(MXU_DTYPE)
    wbb = jnp.pad(w_branch[1].reshape(N_HEADS, HEAD_DIM, -1), ((0, 0), (0, LANES - HEAD_DIM), (0, 0)))
    wbb = wbb.reshape(PAD_W, -1).astype(MXU_DTYPE)
    wb = w_branch.astype(MXU_DTYPE)
    return w_cat, bf, wuq, wuk, wuv, wb[0], wbb, wb[2], wb[3]


def kernel(x, w_in, b_forget, ret_gn_gain, mla_q_norm, mla_kv_norm, w_uq, w_ukv, w_gate, w_branch, w_out,
           g_pre_mix, g_post_mix, g_pre_ffn, g_post_ffn, w_ffn_gate, w_ffn_up, w_ffn_down):
    B, S, D = x.shape
    depth = w_in.shape[0]
    assert S % (ROW_TILE * 1) == 0 and S % (DIL_PATTERNS[-1][1] * DIL_BLOCK) == 0
    ret_tabs = _rope_tables(S, HEAD_DIM // 2, 0)
    mla_tabs = _rope_tables(S, MLA_ROPE // 2, MLA_NOPE)
    r1 = lambda v: v.reshape(1, -1)
    for l in range(depth):
        w_cat, bf, wuq, wuk, wuv, wba, wbb, wbc, wbd = _layer_weights(
            w_in[l], b_forget[l], w_uq[l], w_ukv[l], w_branch[l])
        (fq, fk, fva, c, ct, dq, dk, dva, rq, rk, rv, sg, mq, mk, mva) = _inproj(
            x, r1(g_pre_mix[l]), w_cat, bf, ret_tabs, mla_tabs,
            r1(mla_q_norm[l]), r1(mla_kv_norm[l]), wuq, wuk, wuv)
        o_a = _flash(fq, fk, fva, c, ct)
        d_acc = [_dilated(dq, dk, dva, dil) for _, dil in DIL_PATTERNS]
        o_c = _retention(rq, rk, rv, sg, r1(ret_gn_gain[l]))
        o_d = _flash(mq, mk, mva)
        flat = lambda a: a.reshape(B * S, a.shape[-1])
        x2 = _merge(flat(x), r1(g_pre_mix[l]), flat(o_a), *[flat(a) for a in d_acc], flat(o_c), flat(o_d),
                    w_gate[l].astype(MXU_DTYPE), wba, wbb, wbc, wbd, w_out[l].astype(MXU_DTYPE),
                    r1(g_post_mix[l]))
        x2 = _ffn(x2, r1(g_pre_ffn[l]), w_ffn_gate[l].astype(MXU_DTYPE), w_ffn_up[l].astype(MXU_DTYPE),
                  w_ffn_down[l].astype(MXU_DTYPE), r1(g_post_ffn[l]))
        x = x2.reshape(B, S, D)
    return x
```

```python
import functools
import math

import jax
import jax.numpy as jnp
import numpy as np
from jax import lax
from jax.experimental import pallas as pl
from jax.experimental.pallas import tpu as pltpu

HEAD_DIM = 64
N_HEADS = 4
PAIR_W = N_HEADS * HEAD_DIM
LANES = 128
PAD_W = N_HEADS * LANES
DIL_PATTERNS = ((128, 1), (512, 4), (2048, 16))
DIL_BLOCK = 128
DIL_UNROLL = 4
RET_CHUNK = 128
MLA_Q_RANK = 256
MLA_KV_RANK = 128
MLA_NOPE = 64
MLA_ROPE = 32
MLA_V = 64
ROPE_THETA = 10000.0
RMS_EPS = 1e-6
GN_EPS = 1e-5
NEG_BIG = -1e30
LOG2E = math.log2(math.e)

MXU_DTYPE = jnp.bfloat16
ROW_TILE = 512
VMEM_LIMIT = 56 * 1024 * 1024

_C_FQ, _C_FK, _C_FV, _C_FF = 0, 256, 512, 1024
_C_DQ, _C_DK, _C_DV = 1152, 1408, 1664
_C_RQ, _C_RK, _C_RV, _C_RG = 2176, 2432, 2688, 2944
_C_CQ, _C_CKV, _C_KR = 3200, 3456, 3584
IN_CAT_W = 3712


def _f32(x):
    return x.astype(jnp.float32)


def _rms(x, g):
    ms = jnp.mean(x * x, axis=-1, keepdims=True)
    return x * lax.rsqrt(ms + RMS_EPS) * g


def _mm(a, b):
    return jnp.dot(a, b, preferred_element_type=jnp.float32)


def _mm_nt(a, b):
    return lax.dot_general(a, b, (((1,), (1,)), ((), ())), preferred_element_type=jnp.float32)


def _mm_tn(a, b):
    return lax.dot_general(a, b, (((0,), (0,)), ((), ())), preferred_element_type=jnp.float32)


def _lane(shape):
    return lax.broadcasted_iota(jnp.int32, shape, len(shape) - 1)


def _rope_chunks(a, cos, sin_a, sin_b, half):
    outs = []
    for c in range(a.shape[1] // LANES):
        t = a[:, c * LANES:(c + 1) * LANES]
        up = pltpu.roll(t, LANES - half, axis=1)
        dn = pltpu.roll(t, half, axis=1)
        outs.append(t * cos + up * sin_a + dn * sin_b)
    return outs[0] if len(outs) == 1 else jnp.concatenate(outs, axis=1)


def _inproj_kernel(x_ref, g_ref, w_ref, bf_ref, rcos_ref, rsa_ref, rsb_ref, mcos_ref, msa_ref, msb_ref,
                   qn_ref, kvn_ref, wuq_ref, wuk_ref, wuv_ref,
                   fq_ref, fk_ref, fva_ref,
                   dq_ref, dk_ref, dva_ref,
                   rq_ref, rk_ref, rv_ref, sg_ref,
                   mq_ref, mk_ref, mva_ref,
                   carry_ref, *, mla_scale):
    tm = x_ref.shape[0]
    h = _rms(x_ref[...], g_ref[...]).astype(MXU_DTYPE)

    def proj(lo, width):
        return _mm(h, w_ref[:, lo:lo + width])

    ones_up = jnp.where(_lane((1, PAD_W)) % LANES >= HEAD_DIM, 1.0, 0.0)

    fq_ref[...] = (proj(_C_FQ, PAIR_W) * (HEAD_DIM ** -0.5 * LOG2E)).astype(fq_ref.dtype)
    fk_ref[:, 0:PAIR_W] = proj(_C_FK, PAIR_W).astype(fk_ref.dtype)
    fva_ref[...] = (proj(_C_FV, PAD_W) + ones_up).astype(fva_ref.dtype)
    ff = proj(_C_FF, LANES) + bf_ref[...]
    log_f = jnp.minimum(ff, 0.0) - jnp.log1p(jnp.exp(-jnp.abs(ff)))
    rows = lax.broadcasted_iota(jnp.int32, (tm, LANES), 0)
    cs = log_f
    step = 1
    while step < tm:
        cs = cs + jnp.where(rows >= step, pltpu.roll(cs, step, axis=0), 0.0)
        step *= 2

    @pl.when(pl.program_id(1) == 0)
    def _():
        carry_ref[...] = jnp.zeros_like(carry_ref)

    cs = cs + carry_ref[...]
    carry_ref[...] = cs[tm - 1:tm, :]
    cs2 = cs * LOG2E
    hi = _f32(cs2.astype(MXU_DTYPE))
    rem = cs2 - hi
    mid = _f32(rem.astype(MXU_DTYPE))
    lo = rem - mid
    lane = _lane(cs2.shape)
    split = jnp.where(lane < N_HEADS, hi,
                      jnp.where(lane < 2 * N_HEADS, pltpu.roll(mid, N_HEADS, axis=1),
                                jnp.where(lane < 3 * N_HEADS, pltpu.roll(lo, 2 * N_HEADS, axis=1), 0.0)))
    fk_ref[:, PAIR_W:PAIR_W + LANES] = (-split).astype(fk_ref.dtype)

    dq_ref[...] = (proj(_C_DQ, PAIR_W) * (HEAD_DIM ** -0.5 * LOG2E)).astype(dq_ref.dtype)
    dk_ref[...] = proj(_C_DK, PAIR_W).astype(dk_ref.dtype)
    dva_ref[...] = (proj(_C_DV, PAD_W) + ones_up).astype(dva_ref.dtype)

    rcos, rsa, rsb = rcos_ref[...], rsa_ref[...], rsb_ref[...]
    rq_ref[...] = _rope_chunks(proj(_C_RQ, PAIR_W), rcos, rsa, rsb, HEAD_DIM // 2).astype(rq_ref.dtype)
    rk_ref[...] = _rope_chunks(proj(_C_RK, PAIR_W), rcos, rsa, rsb, HEAD_DIM // 2) * HEAD_DIM ** -0.5
    rv_ref[...] = proj(_C_RV, PAIR_W).astype(rv_ref.dtype)
    rg = proj(_C_RG, PAIR_W)
    sg_ref[...] = rg / (1.0 + jnp.exp(-rg))

    mcos, msa, msb = mcos_ref[...], msa_ref[...], msb_ref[...]
    cq = _rms(proj(_C_CQ, MLA_Q_RANK), qn_ref[...]).astype(MXU_DTYPE)
    q = _rope_chunks(_mm(cq, wuq_ref[...]), mcos, msa, msb, MLA_ROPE // 2)
    mq_ref[...] = (q * mla_scale).astype(mq_ref.dtype)
    ckv = _rms(proj(_C_CKV, MLA_KV_RANK), kvn_ref[...]).astype(MXU_DTYPE)
    kr = _rope_chunks(proj(_C_KR, LANES), mcos, msa, msb, MLA_ROPE // 2)
    mk_ref[...] = (_mm(ckv, wuk_ref[...]) + jnp.concatenate([kr] * N_HEADS, axis=1)).astype(mk_ref.dtype)
    mva_ref[...] = (_mm(ckv, wuv_ref[...]) + ones_up).astype(mva_ref.dtype)


def _inproj(x, g, w_cat, bf, ret_tabs, mla_tabs, qn, kvn, wuq, wuk, wuv):
    B, S, D = x.shape
    tm = ROW_TILE
    ns = S // tm
    row = lambda w: pl.BlockSpec((None, tm, w), lambda b, i: (b, i, 0))
    full = lambda a: pl.BlockSpec(a.shape, lambda b, i: (0,) * a.ndim)
    tab = pl.BlockSpec((tm, LANES), lambda b, i: (i, 0))
    bf16, f32 = MXU_DTYPE, jnp.float32
    out_defs = [
        (PAIR_W, bf16), (PAIR_W + LANES, bf16), (PAD_W, bf16),
        (PAIR_W, bf16), (PAIR_W, bf16), (PAD_W, bf16),
        (PAIR_W, bf16), (PAIR_W, f32), (PAIR_W, bf16), (PAIR_W, f32),
        (PAD_W, bf16), (PAD_W, bf16), (PAD_W, bf16),
    ]
    out_shape, out_specs = [], []
    for d in out_defs:
        out_shape.append(jax.ShapeDtypeStruct((B, S, d[0]), d[1]))
        out_specs.append(row(d[0]))
    consts = [g, w_cat, bf]
    tabs = list(ret_tabs) + list(mla_tabs)
    tail = [qn, kvn, wuq, wuk, wuv]
    return pl.pallas_call(
        functools.partial(_inproj_kernel, mla_scale=(MLA_NOPE + MLA_ROPE) ** -0.5 * LOG2E),
        grid=(B, ns),
        in_specs=[row(D)] + [full(a) for a in consts] + [tab] * 6 + [full(a) for a in tail],
        out_specs=out_specs,
        out_shape=out_shape,
        scratch_shapes=[pltpu.VMEM((1, LANES), f32)],
        compiler_params=pltpu.CompilerParams(
            dimension_semantics=("arbitrary", "arbitrary"), vmem_limit_bytes=VMEM_LIMIT),
        name="inproj",
    )(x, *consts, *tabs, *tail)


def _head_q(q_ref, h, pair_q):
    if pair_q:
        g, e = divmod(h, 2)
        q2 = q_ref[:, g * LANES:(g + 1) * LANES]
        keep = (_lane(q2.shape) >= HEAD_DIM) if e else (_lane(q2.shape) < HEAD_DIM)
        return jnp.where(keep, q2, jnp.zeros_like(q2)), g
    return q_ref[:, h * LANES:(h + 1) * LANES], h


def _pack_pair(o_even, o_odd):
    return jnp.where(_lane(o_even.shape) < HEAD_DIM, o_even, pltpu.roll(o_odd, HEAD_DIM, axis=1))


def _flash_kernel(q_ref, k_ref, va_ref, o_ref, qop_ref, m_ref, acc_ref, *, fox):
    tq = q_ref.shape[0]
    tk = tq
    qi = pl.program_id(1)
    for h in range(N_HEADS):
        if fox:
            qm, _ = _head_q(q_ref, h, True)
            lane = _lane(qm.shape)
            ones = jnp.where((lane % N_HEADS == h) & (lane < 3 * N_HEADS), 1.0, 0.0).astype(qm.dtype)
            qop_ref[h] = jnp.concatenate([qm, ones], axis=1)
        else:
            qop_ref[h] = q_ref[:, h * LANES:(h + 1) * LANES]
    m_ref[...] = jnp.full(m_ref.shape, NEG_BIG, jnp.float32)
    acc_ref[...] = jnp.zeros(acc_ref.shape, jnp.float32)

    def keys(start, h):
        if fox:
            g = h // 2
            return jnp.concatenate([k_ref[pl.ds(start, tk), g * LANES:(g + 1) * LANES],
                                    k_ref[pl.ds(start, tk), PAIR_W:PAIR_W + LANES]], axis=1)
        return k_ref[pl.ds(start, tk), h * LANES:(h + 1) * LANES]

    def step(j, diagonal):
        start = pl.multiple_of(j * tk, tk)
        for h in range(N_HEADS):
            s = _mm_nt(qop_ref[h], keys(start, h))
            if diagonal:
                r = lax.broadcasted_iota(jnp.int32, s.shape, 0)
                c = lax.broadcasted_iota(jnp.int32, s.shape, 1)
                s = jnp.where(c <= r, s, NEG_BIG)
            m = m_ref[h]
            m_new = jnp.maximum(m, jnp.max(s, axis=-1, keepdims=True))
            p = jnp.concatenate([jnp.exp2(s[:, c * LANES:(c + 1) * LANES] - m_new) for c in range(tk // LANES)],
                                axis=1).astype(MXU_DTYPE)
            acc_ref[h] = (jnp.exp2(m - m_new) * acc_ref[h]
                          + _mm(p, va_ref[pl.ds(start, tk), h * LANES:(h + 1) * LANES]))
            m_ref[h] = m_new

    def body(j, carry):
        step(j, False)
        return carry

    lax.fori_loop(0, qi, body, 0)
    step(qi, True)
    outs = []
    for h in range(N_HEADS):
        acc = acc_ref[h]
        outs.append(acc / pltpu.roll(acc, HEAD_DIM, axis=1))
    for g in range(N_HEADS // 2):
        o_ref[:, g * LANES:(g + 1) * LANES] = _pack_pair(outs[2 * g], outs[2 * g + 1]).astype(o_ref.dtype)


def _flash(q, k, va, fox):
    B, S, qw = q.shape
    tq = ROW_TILE
    seq = lambda w: pl.BlockSpec((None, S, w), lambda b, i: (b, 0, 0))
    return pl.pallas_call(
        functools.partial(_flash_kernel, fox=fox),
        grid=(B, S // tq),
        in_specs=[pl.BlockSpec((None, tq, qw), lambda b, i: (b, i, 0)), seq(k.shape[-1]), seq(PAD_W)],
        out_specs=pl.BlockSpec((None, tq, PAIR_W), lambda b, i: (b, i, 0)),
        out_shape=jax.ShapeDtypeStruct((B, S, PAIR_W), MXU_DTYPE),
        scratch_shapes=[pltpu.VMEM((N_HEADS, tq, 2 * LANES if fox else LANES), MXU_DTYPE),
                        pltpu.VMEM((N_HEADS, tq, LANES), jnp.float32),
                        pltpu.VMEM((N_HEADS, tq, LANES), jnp.float32)],
        compiler_params=pltpu.CompilerParams(
            dimension_semantics=("arbitrary", "arbitrary"), vmem_limit_bytes=VMEM_LIMIT),
        name="fox_attn" if fox else "mla_attn",
    )(q, k, va)


def _dilated_kernel(q_ref, k_ref, va_ref, o_ref, qf_ref, kf_ref, vf_ref, acc_ref):
    S = q_ref.shape[0]
    n = DIL_BLOCK
    tile = ROW_TILE
    keep_stats = _lane((n, LANES)) < HEAD_DIM + HEAD_DIM // 2
    stat0 = jnp.where(_lane((tile, LANES)) < HEAD_DIM + HEAD_DIM // 2, 0.0, NEG_BIG)

    def widen(i, carry):
        rows = pl.ds(pl.multiple_of(i * tile, tile), tile)
        for src, dst in ((q_ref, qf_ref), (k_ref, kf_ref), (va_ref, vf_ref)):
            for c in range(dst.shape[0]):
                dst[c, rows, :] = _f32(src[rows, c * LANES:(c + 1) * LANES])
        for h in range(N_HEADS):
            acc_ref[h, rows, :] = stat0
        return carry

    lax.fori_loop(0, S // tile, widen, 0)

    diff = lax.broadcasted_iota(jnp.int32, (n, 2 * n), 0) - lax.broadcasted_iota(jnp.int32, (n, 2 * n), 1)
    for _, dil in DIL_PATTERNS:
        shift = dil.bit_length() - 1

        def blocks(t, carry, dil=dil, shift=shift):
            todo = []
            for sub in range(DIL_UNROLL):
                u = t * DIL_UNROLL + sub
                rho = jnp.bitwise_and(u, dil - 1)
                a = jnp.right_shift(u, shift)
                q_rows = pl.ds(rho + dil * n * a, n, stride=dil)
                w_rows = pl.ds(rho + dil * n * jnp.maximum(a - 1, 0), 2 * n, stride=dil)
                dist = diff + jnp.where(a == 0, 0, n)
                valid = (dist >= 0) & (dist <= n)
                q2 = [qf_ref[g, q_rows, :].astype(MXU_DTYPE) for g in range(N_HEADS // 2)]
                k2 = [kf_ref[g, w_rows, :].astype(MXU_DTYPE) for g in range(N_HEADS // 2)]
                v4 = [vf_ref[h, w_rows, :].astype(MXU_DTYPE) for h in range(N_HEADS)]
                old = [acc_ref[h, q_rows, :] for h in range(N_HEADS)]
                m_old = [jnp.broadcast_to(o[:, LANES - 1:LANES], (n, LANES)) for o in old]
                todo.append((q_rows, valid, q2, k2, v4, old, m_old))
            done = []
            for q_rows, valid, q2, k2, v4, old, m_old in todo:
                for h in range(N_HEADS):
                    g, e = divmod(h, 2)
                    keep = (_lane(q2[g].shape) >= HEAD_DIM) if e else (_lane(q2[g].shape) < HEAD_DIM)
                    qh = jnp.where(keep, q2[g], jnp.zeros_like(q2[g]))
                    s = jnp.where(valid, _mm_nt(qh, k2[g]), NEG_BIG)
                    m_new = jnp.maximum(m_old[h], jnp.max(s, axis=-1, keepdims=True))
                    p = jnp.concatenate([jnp.exp2(s[:, c * LANES:(c + 1) * LANES] - m_new) for c in range(2)],
                                        axis=1).astype(MXU_DTYPE)
                    upd = jnp.exp2(m_old[h] - m_new) * old[h] + _mm(p, v4[h])
                    done.append((h, q_rows, jnp.where(keep_stats, upd, m_new)))
            for h, q_rows, val in done:
                acc_ref[h, q_rows, :] = val
            return carry

        lax.fori_loop(0, S // n // DIL_UNROLL, blocks, 0)

    def finish(i, carry):
        rows = pl.ds(pl.multiple_of(i * tile, tile), tile)
        outs = []
        for h in range(N_HEADS):
            a_h = acc_ref[h, rows, :]
            outs.append(a_h / a_h[:, HEAD_DIM:HEAD_DIM + 1])
        for g in range(N_HEADS // 2):
            o_ref[rows, g * LANES:(g + 1) * LANES] = _pack_pair(outs[2 * g], outs[2 * g + 1]).astype(o_ref.dtype)
        return carry

    lax.fori_loop(0, S // tile, finish, 0)


def _dilated(q, k, va):
    B, S, _ = q.shape
    assert S % (DIL_PATTERNS[-1][1] * 2 * DIL_BLOCK) == 0 and S % ROW_TILE == 0
    seq = lambda w: pl.BlockSpec((None, S, w), lambda b: (b, 0, 0))
    f32 = jnp.float32
    return pl.pallas_call(
        _dilated_kernel,
        grid=(B,),
        in_specs=[seq(PAIR_W), seq(PAIR_W), seq(PAD_W)],
        out_specs=seq(PAIR_W),
        out_shape=jax.ShapeDtypeStruct((B, S, PAIR_W), MXU_DTYPE),
        scratch_shapes=[pltpu.VMEM((PAIR_W // LANES, S, LANES), f32), pltpu.VMEM((PAIR_W // LANES, S, LANES), f32),
                        pltpu.VMEM((N_HEADS, S, LANES), f32), pltpu.VMEM((N_HEADS, S, LANES), f32)],
        compiler_params=pltpu.CompilerParams(dimension_semantics=("arbitrary",), vmem_limit_bytes=VMEM_LIMIT),
        name="dilated",
    )(q, k, va)


def _retention_kernel(q_ref, k_ref, v_ref, sg_ref, gain_ref, o_ref, state_ref):
    C = RET_CHUNK
    T = q_ref.shape[0]

    @pl.when(pl.program_id(1) == 0)
    def _():
        state_ref[...] = jnp.zeros_like(state_ref)

    lane = _lane((C, LANES))
    low = lane < HEAD_DIM
    pos = _f32(lax.broadcasted_iota(jnp.int32, (C, LANES), 0))
    rel = pos - _f32(lane)
    row_low = lax.broadcasted_iota(jnp.int32, (LANES, LANES), 0) < HEAD_DIM
    same_head = row_low == (_lane((LANES, LANES)) < HEAD_DIM)
    for g in range(N_HEADS // 2):
        lg = [math.log1p(-(2.0 ** (-5.0 - (2 * g + e)))) for e in range(2)]
        lg_lane = jnp.where(low, lg[0], lg[1])
        xi = jnp.exp((pos + 1.0) * lg_lane)
        zeta = jnp.exp((C - 1.0 - pos) * lg_lane)
        g_rows = jnp.where(row_low, math.exp(C * lg[0]), math.exp(C * lg[1]))
        d_in = [jnp.where(rel >= 0, jnp.exp(jnp.maximum(rel, 0.0) * lg[e]), 0.0) for e in range(2)]
        cols = slice(g * LANES, (g + 1) * LANES)
        R = state_ref[g]
        for ch in range(T // C):
            rows = slice(ch * C, (ch + 1) * C)
            q2 = q_ref[rows, cols]
            kf = k_ref[rows, cols]
            k2 = kf.astype(MXU_DTYPE)
            v2 = v_ref[rows, cols]
            inner = []
            for e in range(2):
                qm = jnp.where(low if e == 0 else ~low, q2, jnp.zeros_like(q2))
                sc = _mm_nt(qm, k2) * d_in[e]
                inner.append(_mm(sc.astype(MXU_DTYPE), v2))
            cross = _mm(q2, R.astype(MXU_DTYPE)) * xi
            out = jnp.where(low, inner[0], inner[1]) + cross
            kv = _mm_tn((kf * zeta).astype(MXU_DTYPE), v2)
            R = g_rows * R + jnp.where(same_head, kv, 0.0)
            mu = jnp.where(low,
                           jnp.sum(jnp.where(low, out, 0.0), axis=-1, keepdims=True),
                           jnp.sum(jnp.where(low, 0.0, out), axis=-1, keepdims=True)) * (1.0 / HEAD_DIM)
            dlt = out - mu
            sq = dlt * dlt
            var = jnp.where(low,
                            jnp.sum(jnp.where(low, sq, 0.0), axis=-1, keepdims=True),
                            jnp.sum(jnp.where(low, 0.0, sq), axis=-1, keepdims=True)) * (1.0 / HEAD_DIM)
            y = dlt * lax.rsqrt(var + GN_EPS) * gain_ref[:, cols]
            o_ref[rows, cols] = (sg_ref[rows, cols] * y).astype(o_ref.dtype)
        state_ref[g] = R


def _retention(rq, rk, rv, sg, gain):
    B, S, _ = rq.shape
    T = ROW_TILE
    row = pl.BlockSpec((None, T, PAIR_W), lambda b, i: (b, i, 0))
    return pl.pallas_call(
        _retention_kernel,
        grid=(B, S // T),
        in_specs=[row, row, row, row, pl.BlockSpec((1, PAIR_W), lambda b, i: (0, 0))],
        out_specs=row,
        out_shape=jax.ShapeDtypeStruct((B, S, PAIR_W), MXU_DTYPE),
        scratch_shapes=[pltpu.VMEM((N_HEADS // 2, LANES, LANES), jnp.float32)],
        compiler_params=pltpu.CompilerParams(
            dimension_semantics=("arbitrary", "arbitrary"), vmem_limit_bytes=VMEM_LIMIT),
        name="retention",
    )(rq, rk, rv, sg, gain)


def _merge_kernel(x_ref, gpre_ref, oa_ref, ob_ref, oc_ref, od_ref, wg_ref, wb_ref, wo_ref, gpost_ref, y_ref):
    D = x_ref.shape[1]
    x = x_ref[...]
    h = _rms(x, gpre_ref[...]).astype(MXU_DTYPE)
    merged = None
    for n, o_ref in enumerate((oa_ref, ob_ref, oc_ref, od_ref)):
        gate = 1.0 / (1.0 + jnp.exp(-_mm(h, wg_ref[:, n * D:(n + 1) * D])))
        term = _mm(o_ref[...], wb_ref[n]) * gate
        merged = term if merged is None else merged + term
    mix = _mm(merged.astype(MXU_DTYPE), wo_ref[...])
    y_ref[...] = x + _rms(mix, gpost_ref[...])


def _merge(x2, gpre, oa, ob, oc, od, wg, wb, wo, gpost):
    N, D = x2.shape
    tm = ROW_TILE
    row = lambda w: pl.BlockSpec((tm, w), lambda i: (i, 0))
    full = lambda a: pl.BlockSpec(a.shape, lambda i: (0,) * a.ndim, pipeline_mode=pl.Buffered(1))
    return pl.pallas_call(
        _merge_kernel,
        grid=(N // tm,),
        in_specs=[row(D), full(gpre), row(PAIR_W), row(PAIR_W), row(PAIR_W), row(PAIR_W),
                  full(wg), full(wb), full(wo), full(gpost)],
        out_specs=row(D),
        out_shape=jax.ShapeDtypeStruct((N, D), jnp.float32),
        compiler_params=pltpu.CompilerParams(dimension_semantics=("arbitrary",), vmem_limit_bytes=VMEM_LIMIT),
        name="merge",
    )(x2, gpre, oa, ob, oc, od, wg, wb, wo, gpost)


def _ffn_kernel(x_ref, gpre_ref, wgate_ref, wup_ref, wdown_ref, gpost_ref, y_ref):
    x = x_ref[...]
    h = _rms(x, gpre_ref[...]).astype(MXU_DTYPE)
    gate = _mm(h, wgate_ref[...])
    act = (gate / (1.0 + jnp.exp(-gate)) * _mm(h, wup_ref[...])).astype(MXU_DTYPE)
    f = _mm(act, wdown_ref[...])
    y_ref[...] = x + _rms(f, gpost_ref[...])


def _ffn(x2, gpre, wgate, wup, wdown, gpost):
    N, D = x2.shape
    tm = ROW_TILE
    row = pl.BlockSpec((tm, D), lambda i: (i, 0))
    full = lambda a: pl.BlockSpec(a.shape, lambda i: (0,) * a.ndim, pipeline_mode=pl.Buffered(1))
    return pl.pallas_call(
        _ffn_kernel,
        grid=(N // tm,),
        in_specs=[row, full(gpre), full(wgate), full(wup), full(wdown), full(gpost)],
        out_specs=row,
        out_shape=jax.ShapeDtypeStruct((N, D), jnp.float32),
        compiler_params=pltpu.CompilerParams(dimension_semantics=("arbitrary",), vmem_limit_bytes=VMEM_LIMIT),
        name="ffn",
    )(x2, gpre, wgate, wup, wdown, gpost)


def _rope_tables(S, half, lo):
    period = 2 * half if lo == 0 else LANES
    inv = ROPE_THETA ** (-jnp.arange(half, dtype=jnp.float32) / half)
    ang = jnp.arange(S, dtype=jnp.float32)[:, None] * inv[None, :]
    cos, sin = jnp.cos(ang), jnp.sin(ang)
    reps = LANES // period
    pad_lo = jnp.zeros((S, lo), jnp.float32)
    pad_hi = jnp.zeros((S, period - lo - 2 * half), jnp.float32)
    zeros = jnp.zeros((S, half), jnp.float32)
    cos_t = jnp.concatenate([pad_lo + 1.0, cos, cos, pad_hi + 1.0], axis=1)
    sin_up = jnp.concatenate([pad_lo, -sin, zeros, pad_hi], axis=1)
    sin_dn = jnp.concatenate([pad_lo, zeros, sin, pad_hi], axis=1)
    return tuple(jnp.tile(t, (1, reps)) for t in (cos_t, sin_up, sin_dn))


def _pad_heads(w, width=HEAD_DIM):
    K = w.shape[0]
    w = w.reshape(K, N_HEADS, width)
    return jnp.pad(w, ((0, 0), (0, 0), (0, LANES - width))).reshape(K, PAD_W)


def _layer_weights(w_in, b_forget, w_uq, w_ukv, w_branch):
    sizes = (PAIR_W, PAIR_W, PAIR_W, N_HEADS, PAIR_W, PAIR_W, PAIR_W, PAIR_W, PAIR_W, PAIR_W, PAIR_W,
             MLA_Q_RANK, MLA_KV_RANK, MLA_ROPE)
    offs = np.cumsum(sizes)[:-1]
    fq, fk, fv, ff, dq, dk, dv, rq, rk, rv, rg, cq, ckv, kr = jnp.split(w_in, offs, axis=1)
    D = w_in.shape[0]
    ff_p = jnp.pad(ff, ((0, 0), (0, LANES - N_HEADS)))
    kr_p = jnp.pad(kr, ((0, 0), (MLA_NOPE, LANES - MLA_NOPE - MLA_ROPE)))
    w_cat = jnp.concatenate([fq, fk, _pad_heads(fv), ff_p, dq, dk, _pad_heads(dv), rq, rk, rv, rg, cq, ckv, kr_p],
                            axis=1).astype(MXU_DTYPE)
    assert w_cat.shape == (D, IN_CAT_W)
    bf = jnp.pad(b_forget, (0, LANES - N_HEADS)).reshape(1, LANES)
    wuq = _pad_heads(w_uq, MLA_NOPE + MLA_ROPE).astype(MXU_DTYPE)
    ukv = w_ukv.reshape(MLA_KV_RANK, N_HEADS, MLA_NOPE + MLA_V)
    wuk = _pad_heads(ukv[:, :, :MLA_NOPE].reshape(MLA_KV_RANK, -1)).astype(MXU_DTYPE)
    wuv = _pad_heads(ukv[:, :, MLA_NOPE:].reshape(MLA_KV_RANK, -1)).astype(MXU_DTYPE)
    return w_cat, bf, wuq, wuk, wuv, w_branch.astype(MXU_DTYPE)


def kernel(x, w_in, b_forget, ret_gn_gain, mla_q_norm, mla_kv_norm, w_uq, w_ukv, w_gate, w_branch, w_out,
           g_pre_mix, g_post_mix, g_pre_ffn, g_post_ffn, w_ffn_gate, w_ffn_up, w_ffn_down):
    B, S, D = x.shape
    depth = w_in.shape[0]
    assert S % ROW_TILE == 0
    ret_tabs = _rope_tables(S, HEAD_DIM // 2, 0)
    mla_tabs = _rope_tables(S, MLA_ROPE // 2, MLA_NOPE)
    r1 = lambda v: v.reshape(1, -1)
    for l in range(depth):
        w_cat, bf, wuq, wuk, wuv, wb = _layer_weights(w_in[l], b_forget[l], w_uq[l], w_ukv[l], w_branch[l])
        (fq, fk, fva, dq, dk, dva, rq, rk, rv, sg, mq, mk, mva) = _inproj(
            x, r1(g_pre_mix[l]), w_cat, bf, ret_tabs, mla_tabs,
            r1(mla_q_norm[l]), r1(mla_kv_norm[l]), wuq, wuk, wuv)
        o_a = _flash(fq, fk, fva, True)
        o_b = _dilated(dq, dk, dva)
        o_c = _retention(rq, rk, rv, sg, r1(ret_gn_gain[l]))
        o_d = _flash(mq, mk, mva, False)
        flat = lambda a: a.reshape(B * S, a.shape[-1])
        x2 = _merge(flat(x), r1(g_pre_mix[l]), flat(o_a), flat(o_b), flat(o_c), flat(o_d),
                    w_gate[l].astype(MXU_DTYPE), wb, w_out[l].astype(MXU_DTYPE), r1(g_post_mix[l]))
        x2 = _ffn(x2, r1(g_pre_ffn[l]), w_ffn_gate[l].astype(MXU_DTYPE), w_ffn_up[l].astype(MXU_DTYPE),
                  w_ffn_down[l].astype(MXU_DTYPE), r1(g_post_ffn[l]))
        x = x2.reshape(B, S, D)
    return x
```

```python
import functools
import math

import jax
import jax.numpy as jnp
import numpy as np
from jax import lax
from jax.experimental import pallas as pl
from jax.experimental.pallas import tpu as pltpu

HEAD_DIM = 64
N_HEADS = 4
PAIR_W = N_HEADS * HEAD_DIM
LANES = 128
PAD_W = N_HEADS * LANES
DIL_PATTERNS = ((128, 1), (512, 4), (2048, 16))
DIL_BLOCK = 128
DIL_UNROLL = 4
RET_CHUNK = 128
MLA_Q_RANK = 256
MLA_KV_RANK = 128
MLA_NOPE = 64
MLA_ROPE = 32
MLA_V = 64
ROPE_THETA = 10000.0
RMS_EPS = 1e-6
GN_EPS = 1e-5
NEG_BIG = -1e30
LOG2E = math.log2(math.e)

MXU_DTYPE = jnp.bfloat16
ROW_TILE = 512
VMEM_LIMIT = 56 * 1024 * 1024

_C_FQ, _C_FK, _C_FV = 0, 256, 512
_C_DQ, _C_DK, _C_DV = 768, 1024, 1280
_C_RQ, _C_RK, _C_RV, _C_RG = 1536, 1792, 2048, 2304
_C_CQ, _C_FFKR = 2560, 2816
IN_CAT_W = 3072


def _f32(x):
    return x.astype(jnp.float32)


def _rms(x, g):
    ms = jnp.mean(x * x, axis=-1, keepdims=True)
    return x * lax.rsqrt(ms + RMS_EPS) * g


def _mm(a, b):
    return jnp.dot(a, b, preferred_element_type=jnp.float32)


def _mm_nt(a, b):
    return lax.dot_general(a, b, (((1,), (1,)), ((), ())), preferred_element_type=jnp.float32)


def _mm_tn(a, b):
    return lax.dot_general(a, b, (((0,), (0,)), ((), ())), preferred_element_type=jnp.float32)


def _lane(shape):
    return lax.broadcasted_iota(jnp.int32, shape, len(shape) - 1)


def _rope_chunks(a, cos, sin_a, sin_b, half):
    outs = []
    for c in range(a.shape[1] // LANES):
        t = a[:, c * LANES:(c + 1) * LANES]
        up = pltpu.roll(t, LANES - half, axis=1)
        dn = pltpu.roll(t, half, axis=1)
        outs.append(t * cos + up * sin_a + dn * sin_b)
    return outs[0] if len(outs) == 1 else jnp.concatenate(outs, axis=1)


def _inproj_kernel(x_ref, g_ref, w_ref, bf_ref, rcos_ref, rsa_ref, rsb_ref, mcos_ref, msa_ref, msb_ref,
                   qn_ref, kvn_ref, wuq_ref, wuk_ref, wuv_ref,
                   fq_ref, fk_ref, fva_ref,
                   dq_ref, dk_ref, dva_ref,
                   rq_ref, rk_ref, rv_ref, sg_ref,
                   mq_ref, mk_ref, mva_ref,
                   carry_ref, wb_ref, *, mla_scale):
    tm = x_ref.shape[0]

    @pl.when((pl.program_id(0) == 0) & (pl.program_id(1) == 0))
    def _():
        for c in range(0, IN_CAT_W, PAIR_W):
            wb_ref[:, c:c + PAIR_W] = w_ref[:, c:c + PAIR_W].astype(wb_ref.dtype)

    @pl.when(pl.program_id(1) == 0)
    def _():
        carry_ref[...] = jnp.zeros_like(carry_ref)

    h = _rms(x_ref[...], g_ref[...]).astype(MXU_DTYPE)

    def proj(lo, width):
        return _mm(h, wb_ref[:, lo:lo + width])

    ones_up = jnp.where(_lane((1, PAD_W)) % LANES >= HEAD_DIM, 1.0, 0.0)
    mcos, msa, msb = mcos_ref[...], msa_ref[...], msb_ref[...]
    rcos, rsa, rsb = rcos_ref[...], rsa_ref[...], rsb_ref[...]

    def with_ones(v):
        out = []
        for hd in range(N_HEADS):
            chunk = v[:, (hd // 2) * LANES:(hd // 2 + 1) * LANES]
            if hd % 2:
                chunk = pltpu.roll(chunk, HEAD_DIM, axis=1)
            out.append(jnp.where(_lane(chunk.shape) < HEAD_DIM, chunk, 1.0))
        return jnp.concatenate(out, axis=1)


    tail = proj(_C_FFKR, 2 * LANES)
    cq_raw = proj(_C_CQ, MLA_Q_RANK)
    ffkr = tail[:, 0:LANES]
    ff = ffkr + bf_ref[...]
    log_f = jnp.minimum(ff, 0.0) - jnp.log1p(jnp.exp(-jnp.abs(ff)))
    rows = lax.broadcasted_iota(jnp.int32, (tm, LANES), 0)
    cs = log_f
    step = 1
    while step < tm:
        cs = cs + jnp.where(rows >= step, pltpu.roll(cs, step, axis=0), 0.0)
        step *= 2
    cs = cs + carry_ref[...]
    carry_ref[...] = cs[tm - 1:tm, :]
    cs2 = cs * LOG2E
    hi = _f32(cs2.astype(MXU_DTYPE))
    rem = cs2 - hi
    mid = _f32(rem.astype(MXU_DTYPE))
    lo = rem - mid
    lane = _lane(cs2.shape)
    split = jnp.where(lane < N_HEADS, hi,
                      jnp.where(lane < 2 * N_HEADS, pltpu.roll(mid, N_HEADS, axis=1),
                                jnp.where(lane < 3 * N_HEADS, pltpu.roll(lo, 2 * N_HEADS, axis=1), 0.0)))
    fk_ref[:, PAIR_W:PAIR_W + LANES] = (-split).astype(fk_ref.dtype)

    rq_ref[...] = _rope_chunks(proj(_C_RQ, PAIR_W), rcos, rsa, rsb, HEAD_DIM // 2).astype(rq_ref.dtype)
    rk_ref[...] = _rope_chunks(proj(_C_RK, PAIR_W), rcos, rsa, rsb, HEAD_DIM // 2) * HEAD_DIM ** -0.5
    rg = proj(_C_RG, PAIR_W)
    sg_ref[...] = rg / (1.0 + jnp.exp(-rg))

    ckv = _rms(tail[:, LANES:2 * LANES], kvn_ref[...]).astype(MXU_DTYPE)
    rope_lanes = (_lane(ffkr.shape) >= MLA_NOPE) & (_lane(ffkr.shape) < MLA_NOPE + MLA_ROPE)
    kr = jnp.where(rope_lanes, _rope_chunks(ffkr, mcos, msa, msb, MLA_ROPE // 2), 0.0)
    mk_ref[...] = (_mm(ckv, wuk_ref[...]) + jnp.concatenate([kr] * N_HEADS, axis=1)).astype(mk_ref.dtype)
    mva_ref[...] = (_mm(ckv, wuv_ref[...]) + ones_up).astype(mva_ref.dtype)
    cq = _rms(cq_raw, qn_ref[...]).astype(MXU_DTYPE)
    q = _rope_chunks(_mm(cq, wuq_ref[...]), mcos, msa, msb, MLA_ROPE // 2)
    mq_ref[...] = (q * mla_scale).astype(mq_ref.dtype)

    fva_ref[...] = with_ones(proj(_C_FV, PAIR_W)).astype(fva_ref.dtype)
    dva_ref[...] = with_ones(proj(_C_DV, PAIR_W)).astype(dva_ref.dtype)
    fq_ref[...] = (proj(_C_FQ, PAIR_W) * (HEAD_DIM ** -0.5 * LOG2E)).astype(fq_ref.dtype)
    dq_ref[...] = (proj(_C_DQ, PAIR_W) * (HEAD_DIM ** -0.5 * LOG2E)).astype(dq_ref.dtype)

    fk_ref[:, 0:PAIR_W] = proj(_C_FK, PAIR_W).astype(fk_ref.dtype)
    dk_ref[...] = proj(_C_DK, PAIR_W).astype(dk_ref.dtype)
    rv_ref[...] = proj(_C_RV, PAIR_W).astype(rv_ref.dtype)


def _inproj(x, g, w_cat, bf, ret_tabs, mla_tabs, qn, kvn, wuq, wuk, wuv):
    B, S, D = x.shape
    tm = ROW_TILE
    ns = S // tm
    row = lambda w: pl.BlockSpec((None, tm, w), lambda b, i: (b, i, 0))
    full = lambda a: pl.BlockSpec(a.shape, lambda b, i: (0,) * a.ndim, pipeline_mode=pl.Buffered(1))
    tab = pl.BlockSpec((tm, LANES), lambda b, i: (i, 0))
    bf16, f32 = MXU_DTYPE, jnp.float32
    out_defs = [
        (PAIR_W, bf16), (PAIR_W + LANES, bf16), (PAD_W, bf16),
        (PAIR_W, bf16), (PAIR_W, bf16), (PAD_W, bf16),
        (PAIR_W, bf16), (PAIR_W, f32), (PAIR_W, bf16), (PAIR_W, f32),
        (PAD_W, bf16), (PAD_W, bf16), (PAD_W, bf16),
    ]
    out_shape, out_specs = [], []
    for d in out_defs:
        out_shape.append(jax.ShapeDtypeStruct((B, S, d[0]), d[1]))
        out_specs.append(row(d[0]))
    consts = [g, w_cat, bf]
    tabs = list(ret_tabs) + list(mla_tabs)
    tail = [qn, kvn, wuq, wuk, wuv]
    return pl.pallas_call(
        functools.partial(_inproj_kernel, mla_scale=(MLA_NOPE + MLA_ROPE) ** -0.5 * LOG2E),
        grid=(B, ns),
        in_specs=[row(D)] + [full(a) for a in consts] + [tab] * 6 + [full(a) for a in tail],
        out_specs=out_specs,
        out_shape=out_shape,
        scratch_shapes=[pltpu.VMEM((1, LANES), f32), pltpu.VMEM(w_cat.shape, bf16)],
        compiler_params=pltpu.CompilerParams(
            dimension_semantics=("arbitrary", "arbitrary"), vmem_limit_bytes=VMEM_LIMIT),
        name="inproj",
    )(x, *consts, *tabs, *tail)


def _head_q(q_ref, h, pair_q):
    if pair_q:
        g, e = divmod(h, 2)
        q2 = q_ref[:, g * LANES:(g + 1) * LANES]
        keep = (_lane(q2.shape) >= HEAD_DIM) if e else (_lane(q2.shape) < HEAD_DIM)
        return jnp.where(keep, q2, jnp.zeros_like(q2)), g
    return q_ref[:, h * LANES:(h + 1) * LANES], h


def _pack_pair(o_even, o_odd):
    return jnp.where(_lane(o_even.shape) < HEAD_DIM, o_even, pltpu.roll(o_odd, HEAD_DIM, axis=1))


def _flash_kernel(q_ref, k_ref, va_ref, o_ref, qop_ref, m_ref, acc_ref, s0_ref, *, fox):
    tq = q_ref.shape[0]
    tk = tq
    qi = pl.program_id(1)
    for h in range(N_HEADS):
        if fox:
            qm, _ = _head_q(q_ref, h, True)
            lane = _lane(qm.shape)
            ones = jnp.where((lane % N_HEADS == h) & (lane < 3 * N_HEADS), 1.0, 0.0).astype(qm.dtype)
            qop_ref[h] = jnp.concatenate([qm, ones], axis=1)
        else:
            qop_ref[h] = q_ref[:, h * LANES:(h + 1) * LANES]
    m_ref[...] = jnp.full(m_ref.shape, NEG_BIG, jnp.float32)
    acc_ref[...] = jnp.zeros(acc_ref.shape, jnp.float32)

    def keys(start, h):
        if fox:
            g = h // 2
            return jnp.concatenate([k_ref[pl.ds(start, tk), g * LANES:(g + 1) * LANES],
                                    k_ref[pl.ds(start, tk), PAIR_W:PAIR_W + LANES]], axis=1)
        return k_ref[pl.ds(start, tk), h * LANES:(h + 1) * LANES]

    def step(j, diagonal):
        start = pl.multiple_of(j * tk, tk)
        s_next = s0_ref[...]
        for h in range(N_HEADS):
            s = s_next
            if h + 1 < N_HEADS:
                s_next = _mm_nt(qop_ref[h + 1], keys(start, h + 1))
            elif not diagonal:
                s0_ref[...] = _mm_nt(qop_ref[0], keys(pl.multiple_of((j + 1) * tk, tk), 0))
            if diagonal:
                r = lax.broadcasted_iota(jnp.int32, s.shape, 0)
                c = lax.broadcasted_iota(jnp.int32, s.shape, 1)
                s = jnp.where(c <= r, s, NEG_BIG)
            m = m_ref[h]
            m_new = jnp.maximum(m, jnp.max(s, axis=-1, keepdims=True))
            p = jnp.concatenate([jnp.exp2(s[:, c * LANES:(c + 1) * LANES] - m_new) for c in range(tk // LANES)],
                                axis=1).astype(MXU_DTYPE)
            acc_ref[h] = (jnp.exp2(m - m_new) * acc_ref[h]
                          + _mm(p, va_ref[pl.ds(start, tk), h * LANES:(h + 1) * LANES]))
            m_ref[h] = m_new

    def body(j, carry):
        step(j, False)
        return carry

    s0_ref[...] = _mm_nt(qop_ref[0], keys(0, 0))
    lax.fori_loop(0, qi, body, 0)
    step(qi, True)
    outs = []
    for h in range(N_HEADS):
        acc = acc_ref[h]
        outs.append(acc / pltpu.roll(acc, HEAD_DIM, axis=1))
    for g in range(N_HEADS // 2):
        o_ref[:, g * LANES:(g + 1) * LANES] = _pack_pair(outs[2 * g], outs[2 * g + 1]).astype(o_ref.dtype)


def _flash(q, k, va, fox):
    B, S, qw = q.shape
    tq = ROW_TILE
    seq = lambda w: pl.BlockSpec((None, S, w), lambda b, i: (b, 0, 0))
    return pl.pallas_call(
        functools.partial(_flash_kernel, fox=fox),
        grid=(B, S // tq),
        in_specs=[pl.BlockSpec((None, tq, qw), lambda b, i: (b, i, 0)), seq(k.shape[-1]), seq(PAD_W)],
        out_specs=pl.BlockSpec((None, tq, PAIR_W), lambda b, i: (b, i, 0)),
        out_shape=jax.ShapeDtypeStruct((B, S, PAIR_W), MXU_DTYPE),
        scratch_shapes=[pltpu.VMEM((N_HEADS, tq, 2 * LANES if fox else LANES), MXU_DTYPE),
                        pltpu.VMEM((N_HEADS, tq, LANES), jnp.float32),
                        pltpu.VMEM((N_HEADS, tq, LANES), jnp.float32),
                        pltpu.VMEM((tq, tq), jnp.float32)],
        compiler_params=pltpu.CompilerParams(
            dimension_semantics=("arbitrary", "arbitrary"), vmem_limit_bytes=VMEM_LIMIT),
        name="fox_attn" if fox else "mla_attn",
    )(q, k, va)


def _dilated_kernel(q_ref, k_ref, va_ref, o_ref, qf_ref, kf_ref, vf_ref, acc_ref, stage_ref):
    S = q_ref.shape[0]
    n = DIL_BLOCK
    tile = ROW_TILE
    quarter = S // 4
    run = n // 4
    keep_stats = _lane((n, LANES)) < HEAD_DIM + HEAD_DIM // 2
    stat0 = jnp.where(_lane((tile, LANES)) < HEAD_DIM + HEAD_DIM // 2, 0.0, NEG_BIG)

    def widen(i, carry):
        rows = pl.ds(pl.multiple_of(i * tile, tile), tile)
        slot = 0
        for src, dst in ((q_ref, qf_ref), (k_ref, kf_ref), (va_ref, vf_ref)):
            for c in range(dst.shape[0]):
                stage_ref[slot] = _f32(src[rows, c * LANES:(c + 1) * LANES])
                for cls in range(4):
                    dst[c, pl.ds(cls * quarter + i * (tile // 4), tile // 4), :] = (
                        stage_ref[slot, pl.ds(cls, tile // 4, stride=4), :])
                slot += 1
        for h in range(N_HEADS):
            acc_ref[h, rows, :] = stat0
        return carry

    lax.fori_loop(0, S // tile, widen, 0)

    ri = lax.broadcasted_iota(jnp.int32, (n, 2 * n), 0)
    ci = lax.broadcasted_iota(jnp.int32, (n, 2 * n), 1)
    diff_class = ri - ci
    diff_runs = (4 * (jnp.bitwise_and(ri, run - 1) - jnp.bitwise_and(ci, 2 * run - 1))
                 + jnp.right_shift(ri, run.bit_length() - 1) - jnp.right_shift(ci, (2 * run).bit_length() - 1))

    def plan(dil, u):
        if dil == 1:
            first = u == 0
            q_runs = [(cls * quarter + run * u, run, 1) for cls in range(4)]
            w_runs = [(cls * quarter + run * jnp.maximum(u - 1, 0), 2 * run, 1) for cls in range(4)]
            return q_runs, w_runs, diff_runs, first
        if dil == 4:
            cls, a = jnp.bitwise_and(u, 3), jnp.right_shift(u, 2)
            base, step, stride = cls * quarter, n, 1
        else:
            rho, a = jnp.bitwise_and(u, 15), jnp.right_shift(u, 4)
            base, step, stride = jnp.bitwise_and(rho, 3) * quarter + jnp.right_shift(rho, 2), 4 * n, 4
        q_runs = [(base + step * a, n, stride)]
        w_runs = [(base + step * jnp.maximum(a - 1, 0), 2 * n, stride)]
        return q_runs, w_runs, diff_class, a == 0

    def rows_of(r):
        start, size, stride = r
        return pl.ds(start, size, stride=stride) if stride > 1 else pl.ds(start, size)

    def load(ref, idx, runs):
        parts = [ref[idx, rows_of(r), :] for r in runs]
        return parts[0] if len(parts) == 1 else jnp.concatenate(parts, axis=0)

    def store(ref, idx, runs, val):
        at = 0
        for r in runs:
            ref[idx, rows_of(r), :] = val[at:at + r[1], :]
            at += r[1]

    for _, dil in DIL_PATTERNS:

        def blocks(t, carry, dil=dil):
            todo = []
            for sub in range(DIL_UNROLL):
                q_runs, w_runs, diff, first = plan(dil, t * DIL_UNROLL + sub)
                dist = diff + jnp.where(first, 0, n)
                valid = (dist >= 0) & (dist <= n)
                q2 = [load(qf_ref, g, q_runs).astype(MXU_DTYPE) for g in range(N_HEADS // 2)]
                k2 = [load(kf_ref, g, w_runs).astype(MXU_DTYPE) for g in range(N_HEADS // 2)]
                v4 = [load(vf_ref, h, w_runs).astype(MXU_DTYPE) for h in range(N_HEADS)]
                old = [load(acc_ref, h, q_runs) for h in range(N_HEADS)]
                m_old = [jnp.broadcast_to(o[:, LANES - 1:LANES], (n, LANES)) for o in old]
                todo.append((q_runs, valid, q2, k2, v4, old, m_old))
            done = []
            for q_runs, valid, q2, k2, v4, old, m_old in todo:
                for h in range(N_HEADS):
                    g, e = divmod(h, 2)
                    keep = (_lane(q2[g].shape) >= HEAD_DIM) if e else (_lane(q2[g].shape) < HEAD_DIM)
                    qh = jnp.where(keep, q2[g], jnp.zeros_like(q2[g]))
                    s = jnp.where(valid, _mm_nt(qh, k2[g]), NEG_BIG)
                    m_new = jnp.maximum(m_old[h], jnp.max(s, axis=-1, keepdims=True))
                    p = jnp.concatenate([jnp.exp2(s[:, c * LANES:(c + 1) * LANES] - m_new) for c in range(2)],
                                        axis=1).astype(MXU_DTYPE)
                    upd = jnp.exp2(m_old[h] - m_new) * old[h] + _mm(p, v4[h])
                    done.append((h, q_runs, jnp.where(keep_stats, upd, m_new)))
            for h, q_runs, val in done:
                store(acc_ref, h, q_runs, val)
            return carry

        lax.fori_loop(0, S // n // DIL_UNROLL, blocks, 0)

    def finish(i, carry):
        rows = pl.ds(pl.multiple_of(i * tile, tile), tile)
        outs = []
        for h in range(N_HEADS):
            for cls in range(4):
                a_h = acc_ref[h, pl.ds(cls * quarter + i * (tile // 4), tile // 4), :]
                stage_ref[h, pl.ds(cls, tile // 4, stride=4), :] = a_h / a_h[:, HEAD_DIM:HEAD_DIM + 1]
            outs.append(stage_ref[h])
        for g in range(N_HEADS // 2):
            o_ref[rows, g * LANES:(g + 1) * LANES] = _pack_pair(outs[2 * g], outs[2 * g + 1]).astype(o_ref.dtype)
        return carry

    lax.fori_loop(0, S // tile, finish, 0)


def _dilated(q, k, va):
    B, S, _ = q.shape
    assert S % (DIL_PATTERNS[-1][1] * 2 * DIL_BLOCK) == 0 and S % ROW_TILE == 0
    seq = lambda w: pl.BlockSpec((None, S, w), lambda b: (b, 0, 0))
    f32 = jnp.float32
    n_chunks = (2 * PAIR_W + PAD_W) // LANES
    return pl.pallas_call(
        _dilated_kernel,
        grid=(B,),
        in_specs=[seq(PAIR_W), seq(PAIR_W), seq(PAD_W)],
        out_specs=seq(PAIR_W),
        out_shape=jax.ShapeDtypeStruct((B, S, PAIR_W), MXU_DTYPE),
        scratch_shapes=[pltpu.VMEM((PAIR_W // LANES, S, LANES), f32), pltpu.VMEM((PAIR_W // LANES, S, LANES), f32),
                        pltpu.VMEM((N_HEADS, S, LANES), f32), pltpu.VMEM((N_HEADS, S, LANES), f32),
                        pltpu.VMEM((n_chunks, ROW_TILE, LANES), f32)],
        compiler_params=pltpu.CompilerParams(dimension_semantics=("arbitrary",), vmem_limit_bytes=VMEM_LIMIT),
        name="dilated",
    )(q, k, va)


def _retention_kernel(q_ref, k_ref, v_ref, sg_ref, gain_ref, o_ref, state_ref):
    C = RET_CHUNK
    T = q_ref.shape[0]

    @pl.when(pl.program_id(1) == 0)
    def _():
        state_ref[...] = jnp.zeros_like(state_ref)

    lane = _lane((C, LANES))
    low = lane < HEAD_DIM
    pos = _f32(lax.broadcasted_iota(jnp.int32, (C, LANES), 0))
    rel = pos - _f32(lane)
    row_low = lax.broadcasted_iota(jnp.int32, (LANES, LANES), 0) < HEAD_DIM
    same_head = row_low == (_lane((LANES, LANES)) < HEAD_DIM)
    for g in range(N_HEADS // 2):
        lg = [math.log1p(-(2.0 ** (-5.0 - (2 * g + e)))) for e in range(2)]
        lg_lane = jnp.where(low, lg[0], lg[1])
        xi = jnp.exp((pos + 1.0) * lg_lane)
        zeta = jnp.exp((C - 1.0 - pos) * lg_lane)
        g_rows = jnp.where(row_low, math.exp(C * lg[0]), math.exp(C * lg[1]))
        d_in = [jnp.where(rel >= 0, jnp.exp(jnp.maximum(rel, 0.0) * lg[e]), 0.0) for e in range(2)]
        cols = slice(g * LANES, (g + 1) * LANES)
        R = state_ref[g]
        for ch in range(T // C):
            rows = slice(ch * C, (ch + 1) * C)
            q2 = q_ref[rows, cols]
            kf = k_ref[rows, cols]
            k2 = kf.astype(MXU_DTYPE)
            v2 = v_ref[rows, cols]
            inner = []
            for e in range(2):
                qm = jnp.where(low if e == 0 else ~low, q2, jnp.zeros_like(q2))
                sc = _mm_nt(qm, k2) * d_in[e]
                inner.append(_mm(sc.astype(MXU_DTYPE), v2))
            cross = _mm(q2, R.astype(MXU_DTYPE)) * xi
            out = jnp.where(low, inner[0], inner[1]) + cross
            kv = _mm_tn((kf * zeta).astype(MXU_DTYPE), v2)
            R = g_rows * R + jnp.where(same_head, kv, 0.0)
            mu = jnp.where(low,
                           jnp.sum(jnp.where(low, out, 0.0), axis=-1, keepdims=True),
                           jnp.sum(jnp.where(low, 0.0, out), axis=-1, keepdims=True)) * (1.0 / HEAD_DIM)
            dlt = out - mu
            sq = dlt * dlt
            var = jnp.where(low,
                            jnp.sum(jnp.where(low, sq, 0.0), axis=-1, keepdims=True),
                            jnp.sum(jnp.where(low, 0.0, sq), axis=-1, keepdims=True)) * (1.0 / HEAD_DIM)
            y = dlt * lax.rsqrt(var + GN_EPS) * gain_ref[:, cols]
            o_ref[rows, cols] = (sg_ref[rows, cols] * y).astype(o_ref.dtype)
        state_ref[g] = R


def _retention(rq, rk, rv, sg, gain):
    B, S, _ = rq.shape
    T = ROW_TILE
    row = pl.BlockSpec((None, T, PAIR_W), lambda b, i: (b, i, 0))
    return pl.pallas_call(
        _retention_kernel,
        grid=(B, S // T),
        in_specs=[row, row, row, row, pl.BlockSpec((1, PAIR_W), lambda b, i: (0, 0))],
        out_specs=row,
        out_shape=jax.ShapeDtypeStruct((B, S, PAIR_W), MXU_DTYPE),
        scratch_shapes=[pltpu.VMEM((N_HEADS // 2, LANES, LANES), jnp.float32)],
        compiler_params=pltpu.CompilerParams(
            dimension_semantics=("arbitrary", "arbitrary"), vmem_limit_bytes=VMEM_LIMIT),
        name="retention",
    )(rq, rk, rv, sg, gain)


def _merge_kernel(x_ref, gpre_ref, oa_ref, ob_ref, oc_ref, od_ref, wg_ref, wb_ref, wo_ref, gpost_ref, y_ref):
    D = x_ref.shape[1]
    x = x_ref[...]
    h = _rms(x, gpre_ref[...]).astype(MXU_DTYPE)
    merged = None
    for n, o_ref in enumerate((oa_ref, ob_ref, oc_ref, od_ref)):
        gate = 1.0 / (1.0 + jnp.exp(-_mm(h, wg_ref[:, n * D:(n + 1) * D])))
        term = _mm(o_ref[...], wb_ref[n]) * gate
        merged = term if merged is None else merged + term
    mix = _mm(merged.astype(MXU_DTYPE), wo_ref[...])
    y_ref[...] = x + _rms(mix, gpost_ref[...])


def _merge(x2, gpre, oa, ob, oc, od, wg, wb, wo, gpost):
    N, D = x2.shape
    tm = ROW_TILE
    row = lambda w: pl.BlockSpec((tm, w), lambda i: (i, 0))
    full = lambda a: pl.BlockSpec(a.shape, lambda i: (0,) * a.ndim, pipeline_mode=pl.Buffered(1))
    return pl.pallas_call(
        _merge_kernel,
        grid=(N // tm,),
        in_specs=[row(D), full(gpre), row(PAIR_W), row(PAIR_W), row(PAIR_W), row(PAIR_W),
                  full(wg), full(wb), full(wo), full(gpost)],
        out_specs=row(D),
        out_shape=jax.ShapeDtypeStruct((N, D), jnp.float32),
        compiler_params=pltpu.CompilerParams(dimension_semantics=("arbitrary",), vmem_limit_bytes=VMEM_LIMIT),
        name="merge",
    )(x2, gpre, oa, ob, oc, od, wg, wb, wo, gpost)


def _ffn_kernel(x_ref, gpre_ref, wgate_ref, wup_ref, wdown_ref, gpost_ref, y_ref):
    x = x_ref[...]
    h = _rms(x, gpre_ref[...]).astype(MXU_DTYPE)
    gate = _mm(h, wgate_ref[...])
    act = (gate / (1.0 + jnp.exp(-gate)) * _mm(h, wup_ref[...])).astype(MXU_DTYPE)
    f = _mm(act, wdown_ref[...])
    y_ref[...] = x + _rms(f, gpost_ref[...])


def _ffn(x2, gpre, wgate, wup, wdown, gpost):
    N, D = x2.shape
    tm = ROW_TILE
    row = pl.BlockSpec((tm, D), lambda i: (i, 0))
    full = lambda a: pl.BlockSpec(a.shape, lambda i: (0,) * a.ndim, pipeline_mode=pl.Buffered(1))
    return pl.pallas_call(
        _ffn_kernel,
        grid=(N // tm,),
        in_specs=[row, full(gpre), full(wgate), full(wup), full(wdown), full(gpost)],
        out_specs=row,
        out_shape=jax.ShapeDtypeStruct((N, D), jnp.float32),
        compiler_params=pltpu.CompilerParams(dimension_semantics=("arbitrary",), vmem_limit_bytes=VMEM_LIMIT),
        name="ffn",
    )(x2, gpre, wgate, wup, wdown, gpost)


def _rope_tables(S, half, lo):
    period = 2 * half if lo == 0 else LANES
    inv = ROPE_THETA ** (-jnp.arange(half, dtype=jnp.float32) / half)
    ang = jnp.arange(S, dtype=jnp.float32)[:, None] * inv[None, :]
    cos, sin = jnp.cos(ang), jnp.sin(ang)
    reps = LANES // period
    pad_lo = jnp.zeros((S, lo), jnp.float32)
    pad_hi = jnp.zeros((S, period - lo - 2 * half), jnp.float32)
    zeros = jnp.zeros((S, half), jnp.float32)
    cos_t = jnp.concatenate([pad_lo + 1.0, cos, cos, pad_hi + 1.0], axis=1)
    sin_up = jnp.concatenate([pad_lo, -sin, zeros, pad_hi], axis=1)
    sin_dn = jnp.concatenate([pad_lo, zeros, sin, pad_hi], axis=1)
    return tuple(jnp.tile(t, (1, reps)) for t in (cos_t, sin_up, sin_dn))


def _pad_heads(w, width=HEAD_DIM):
    K = w.shape[0]
    w = w.reshape(K, N_HEADS, width)
    return jnp.pad(w, ((0, 0), (0, 0), (0, LANES - width))).reshape(K, PAD_W)


def _layer_weights(w_in, b_forget, w_uq, w_ukv, w_branch):
    sizes = (PAIR_W, PAIR_W, PAIR_W, N_HEADS, PAIR_W, PAIR_W, PAIR_W, PAIR_W, PAIR_W, PAIR_W, PAIR_W,
             MLA_Q_RANK, MLA_KV_RANK, MLA_ROPE)
    offs = np.cumsum(sizes)[:-1]
    fq, fk, fv, ff, dq, dk, dv, rq, rk, rv, rg, cq, ckv, kr = jnp.split(w_in, offs, axis=1)
    D = w_in.shape[0]
    ffkr = jnp.concatenate([ff, jnp.zeros((D, MLA_NOPE - N_HEADS), w_in.dtype), kr,
                            jnp.zeros((D, LANES - MLA_NOPE - MLA_ROPE), w_in.dtype)], axis=1)
    w_cat = jnp.concatenate([fq, fk, fv, dq, dk, dv, rq, rk, rv, rg, cq, ffkr, ckv], axis=1)
    assert w_cat.shape == (D, IN_CAT_W)
    bf = jnp.pad(b_forget, (0, LANES - N_HEADS)).reshape(1, LANES)
    wuq = _pad_heads(w_uq, MLA_NOPE + MLA_ROPE).astype(MXU_DTYPE)
    ukv = w_ukv.reshape(MLA_KV_RANK, N_HEADS, MLA_NOPE + MLA_V)
    wuk = _pad_heads(ukv[:, :, :MLA_NOPE].reshape(MLA_KV_RANK, -1)).astype(MXU_DTYPE)
    wuv = _pad_heads(ukv[:, :, MLA_NOPE:].reshape(MLA_KV_RANK, -1)).astype(MXU_DTYPE)
    return w_cat, bf, wuq, wuk, wuv, w_branch.astype(MXU_DTYPE)


def kernel(x, w_in, b_forget, ret_gn_gain, mla_q_norm, mla_kv_norm, w_uq, w_ukv, w_gate, w_branch, w_out,
           g_pre_mix, g_post_mix, g_pre_ffn, g_post_ffn, w_ffn_gate, w_ffn_up, w_ffn_down):
    B, S, D = x.shape
    depth = w_in.shape[0]
    assert S % ROW_TILE == 0
    ret_tabs = _rope_tables(S, HEAD_DIM // 2, 0)
    mla_tabs = _rope_tables(S, MLA_ROPE // 2, MLA_NOPE)
    r1 = lambda v: v.reshape(1, -1)
    for l in range(depth):
        w_cat, bf, wuq, wuk, wuv, wb = _layer_weights(w_in[l], b_forget[l], w_uq[l], w_ukv[l], w_branch[l])
        (fq, fk, fva, dq, dk, dva, rq, rk, rv, sg, mq, mk, mva) = _inproj(
            x, r1(g_pre_mix[l]), w_cat, bf, ret_tabs, mla_tabs,
            r1(mla_q_norm[l]), r1(mla_kv_norm[l]), wuq, wuk, wuv)
        o_a = _flash(fq, fk, fva, True)
        o_b = _dilated(dq, dk, dva)
        o_c = _retention(rq, rk, rv, sg, r1(ret_gn_gain[l]))
        o_d = _flash(mq, mk, mva, False)
        flat = lambda a: a.reshape(B * S, a.shape[-1])
        x2 = _merge(flat(x), r1(g_pre_mix[l]), flat(o_a), flat(o_b), flat(o_c), flat(o_d),
                    w_gate[l].astype(MXU_DTYPE), wb, w_out[l].astype(MXU_DTYPE), r1(g_post_mix[l]))
        x2 = _ffn(x2, r1(g_pre_ffn[l]), w_ffn_gate[l].astype(MXU_DTYPE), w_ffn_up[l].astype(MXU_DTYPE),
                  w_ffn_down[l].astype(MXU_DTYPE), r1(g_post_ffn[l]))
        x = x2.reshape(B, S, D)
    return x
```

```python
import functools
import math

import jax
import jax.numpy as jnp
import numpy as np
from jax import lax
from jax.experimental import pallas as pl
from jax.experimental.pallas import tpu as pltpu

HEAD_DIM = 64
N_HEADS = 4
PAIR_W = N_HEADS * HEAD_DIM
LANES = 128
PAD_W = N_HEADS * LANES
DIL_PATTERNS = ((128, 1), (512, 4), (2048, 16))
DIL_BLOCK = 128
DIL_UNROLL = 8
RET_CHUNK = 128
MLA_Q_RANK = 256
MLA_KV_RANK = 128
MLA_NOPE = 64
MLA_ROPE = 32
MLA_V = 64
ROPE_THETA = 10000.0
RMS_EPS = 1e-6
GN_EPS = 1e-5
NEG_BIG = -1e30
LOG2E = math.log2(math.e)

MXU_DTYPE = jnp.bfloat16
ROW_TILE = 512
VMEM_LIMIT = 56 * 1024 * 1024

_C_FQ, _C_FK, _C_FV = 0, 256, 512
_C_DQ, _C_DK, _C_DV = 768, 1024, 1280
_C_RQ, _C_RK, _C_RV, _C_RG = 1536, 1792, 2048, 2304
_C_CQ, _C_FFKR = 2560, 2816
IN_CAT_W = 3072


def _f32(x):
    return x.astype(jnp.float32)


def _rms(x, g):
    ms = jnp.mean(x * x, axis=-1, keepdims=True)
    return x * lax.rsqrt(ms + RMS_EPS) * g


def _mm(a, b):
    return jnp.dot(a, b, preferred_element_type=jnp.float32)


def _mm_nt(a, b):
    return lax.dot_general(a, b, (((1,), (1,)), ((), ())), preferred_element_type=jnp.float32)


def _mm_tn(a, b):
    return lax.dot_general(a, b, (((0,), (0,)), ((), ())), preferred_element_type=jnp.float32)


def _lane(shape):
    return lax.broadcasted_iota(jnp.int32, shape, len(shape) - 1)


def _rope_chunks(a, cos, sin_a, sin_b, half):
    outs = []
    for c in range(a.shape[1] // LANES):
        t = a[:, c * LANES:(c + 1) * LANES]
        up = pltpu.roll(t, LANES - half, axis=1)
        dn = pltpu.roll(t, half, axis=1)
        outs.append(t * cos + up * sin_a + dn * sin_b)
    return outs[0] if len(outs) == 1 else jnp.concatenate(outs, axis=1)


def _inproj_kernel(x_ref, g_ref, w_ref, bf_ref, rcos_ref, rsa_ref, rsb_ref, mcos_ref, msa_ref, msb_ref,
                   qn_ref, kvn_ref, wuq_ref, wuk_ref, wuv_ref,
                   fq_ref, fk_ref, fva_ref,
                   dq_ref, dk_ref, dva_ref,
                   rq_ref, rk_ref, rv_ref, sg_ref,
                   mq_ref, mk_ref, mva_ref,
                   carry_ref, wb_ref, *, mla_scale):
    tm = x_ref.shape[0]

    @pl.when((pl.program_id(0) == 0) & (pl.program_id(1) == 0))
    def _():
        for c in range(0, IN_CAT_W, PAIR_W):
            wb_ref[:, c:c + PAIR_W] = w_ref[:, c:c + PAIR_W].astype(wb_ref.dtype)

    @pl.when(pl.program_id(1) == 0)
    def _():
        carry_ref[...] = jnp.zeros_like(carry_ref)

    h = _rms(x_ref[...], g_ref[...]).astype(MXU_DTYPE)

    def proj(lo, width):
        return _mm(h, wb_ref[:, lo:lo + width])

    ones_up = jnp.where(_lane((1, PAD_W)) % LANES >= HEAD_DIM, 1.0, 0.0)
    mcos, msa, msb = mcos_ref[...], msa_ref[...], msb_ref[...]
    rcos, rsa, rsb = rcos_ref[...], rsa_ref[...], rsb_ref[...]

    def with_ones(v):
        out = []
        for hd in range(N_HEADS):
            chunk = v[:, (hd // 2) * LANES:(hd // 2 + 1) * LANES]
            if hd % 2:
                chunk = pltpu.roll(chunk, HEAD_DIM, axis=1)
            out.append(jnp.where(_lane(chunk.shape) < HEAD_DIM, chunk, 1.0))
        return jnp.concatenate(out, axis=1)


    tail = proj(_C_FFKR, 2 * LANES)
    cq_raw = proj(_C_CQ, MLA_Q_RANK)
    ffkr = tail[:, 0:LANES]
    ff = ffkr + bf_ref[...]
    log_f = jnp.minimum(ff, 0.0) - jnp.log1p(jnp.exp(-jnp.abs(ff)))
    rows = lax.broadcasted_iota(jnp.int32, (tm, LANES), 0)
    cs = log_f
    step = 1
    while step < tm:
        cs = cs + jnp.where(rows >= step, pltpu.roll(cs, step, axis=0), 0.0)
        step *= 2
    cs = cs + carry_ref[...]
    carry_ref[...] = cs[tm - 1:tm, :]
    cs2 = cs * LOG2E
    hi = _f32(cs2.astype(MXU_DTYPE))
    rem = cs2 - hi
    mid = _f32(rem.astype(MXU_DTYPE))
    lo = rem - mid
    lane = _lane(cs2.shape)
    split = jnp.where(lane < N_HEADS, hi,
                      jnp.where(lane < 2 * N_HEADS, pltpu.roll(mid, N_HEADS, axis=1),
                                jnp.where(lane < 3 * N_HEADS, pltpu.roll(lo, 2 * N_HEADS, axis=1), 0.0)))
    fk_ref[:, PAIR_W:PAIR_W + LANES] = (-split).astype(fk_ref.dtype)

    rq_ref[...] = _rope_chunks(proj(_C_RQ, PAIR_W), rcos, rsa, rsb, HEAD_DIM // 2).astype(rq_ref.dtype)
    rk_ref[...] = _rope_chunks(proj(_C_RK, PAIR_W), rcos, rsa, rsb, HEAD_DIM // 2) * HEAD_DIM ** -0.5
    rg = proj(_C_RG, PAIR_W)
    sg_ref[...] = rg / (1.0 + jnp.exp(-rg))

    ckv = _rms(tail[:, LANES:2 * LANES], kvn_ref[...]).astype(MXU_DTYPE)
    rope_lanes = (_lane(ffkr.shape) >= MLA_NOPE) & (_lane(ffkr.shape) < MLA_NOPE + MLA_ROPE)
    kr = jnp.where(rope_lanes, _rope_chunks(ffkr, mcos, msa, msb, MLA_ROPE // 2), 0.0)
    mk_ref[...] = (_mm(ckv, wuk_ref[...]) + jnp.concatenate([kr] * N_HEADS, axis=1)).astype(mk_ref.dtype)
    mva_ref[...] = (_mm(ckv, wuv_ref[...]) + ones_up).astype(mva_ref.dtype)
    cq = _rms(cq_raw, qn_ref[...]).astype(MXU_DTYPE)
    q = _rope_chunks(_mm(cq, wuq_ref[...]), mcos, msa, msb, MLA_ROPE // 2)
    mq_ref[...] = (q * mla_scale).astype(mq_ref.dtype)

    fva_ref[...] = with_ones(proj(_C_FV, PAIR_W)).astype(fva_ref.dtype)
    dva_ref[...] = with_ones(proj(_C_DV, PAIR_W)).astype(dva_ref.dtype)
    fq_ref[...] = (proj(_C_FQ, PAIR_W) * (HEAD_DIM ** -0.5 * LOG2E)).astype(fq_ref.dtype)
    dq_ref[...] = (proj(_C_DQ, PAIR_W) * (HEAD_DIM ** -0.5 * LOG2E)).astype(dq_ref.dtype)

    fk_ref[:, 0:PAIR_W] = proj(_C_FK, PAIR_W).astype(fk_ref.dtype)
    dk_ref[...] = proj(_C_DK, PAIR_W).astype(dk_ref.dtype)
    rv_ref[...] = proj(_C_RV, PAIR_W).astype(rv_ref.dtype)


def _inproj(x, g, w_cat, bf, ret_tabs, mla_tabs, qn, kvn, wuq, wuk, wuv):
    B, S, D = x.shape
    tm = ROW_TILE
    ns = S // tm
    row = lambda w: pl.BlockSpec((None, tm, w), lambda b, i: (b, i, 0))
    full = lambda a: pl.BlockSpec(a.shape, lambda b, i: (0,) * a.ndim, pipeline_mode=pl.Buffered(1))
    tab = pl.BlockSpec((tm, LANES), lambda b, i: (i, 0))
    bf16, f32 = MXU_DTYPE, jnp.float32
    out_defs = [
        (PAIR_W, bf16), (PAIR_W + LANES, bf16), (PAD_W, bf16),
        (PAIR_W, bf16), (PAIR_W, bf16), (PAD_W, bf16),
        (PAIR_W, bf16), (PAIR_W, f32), (PAIR_W, bf16), (PAIR_W, f32),
        (PAD_W, bf16), (PAD_W, bf16), (PAD_W, bf16),
    ]
    out_shape, out_specs = [], []
    for d in out_defs:
        out_shape.append(jax.ShapeDtypeStruct((B, S, d[0]), d[1]))
        out_specs.append(row(d[0]))
    consts = [g, w_cat, bf]
    tabs = list(ret_tabs) + list(mla_tabs)
    tail = [qn, kvn, wuq, wuk, wuv]
    return pl.pallas_call(
        functools.partial(_inproj_kernel, mla_scale=(MLA_NOPE + MLA_ROPE) ** -0.5 * LOG2E),
        grid=(B, ns),
        in_specs=[row(D)] + [full(a) for a in consts] + [tab] * 6 + [full(a) for a in tail],
        out_specs=out_specs,
        out_shape=out_shape,
        scratch_shapes=[pltpu.VMEM((1, LANES), f32), pltpu.VMEM(w_cat.shape, bf16)],
        compiler_params=pltpu.CompilerParams(
            dimension_semantics=("arbitrary", "arbitrary"), vmem_limit_bytes=VMEM_LIMIT),
        name="inproj",
    )(x, *consts, *tabs, *tail)


def _head_q(q_ref, h, pair_q):
    if pair_q:
        g, e = divmod(h, 2)
        q2 = q_ref[:, g * LANES:(g + 1) * LANES]
        keep = (_lane(q2.shape) >= HEAD_DIM) if e else (_lane(q2.shape) < HEAD_DIM)
        return jnp.where(keep, q2, jnp.zeros_like(q2)), g
    return q_ref[:, h * LANES:(h + 1) * LANES], h


def _pack_pair(o_even, o_odd):
    return jnp.where(_lane(o_even.shape) < HEAD_DIM, o_even, pltpu.roll(o_odd, HEAD_DIM, axis=1))


def _flash_kernel(q_ref, k_ref, va_ref, o_ref, qop_ref, m_ref, acc_ref, s0_ref, *, fox):
    tq = q_ref.shape[0]
    tk = tq
    qi = pl.program_id(1)
    for h in range(N_HEADS):
        if fox:
            qm, _ = _head_q(q_ref, h, True)
            lane = _lane(qm.shape)
            ones = jnp.where((lane % N_HEADS == h) & (lane < 3 * N_HEADS), 1.0, 0.0).astype(qm.dtype)
            qop_ref[h] = jnp.concatenate([qm, ones], axis=1)
        else:
            qop_ref[h] = q_ref[:, h * LANES:(h + 1) * LANES]
    m_ref[...] = jnp.full(m_ref.shape, NEG_BIG, jnp.float32)
    acc_ref[...] = jnp.zeros(acc_ref.shape, jnp.float32)

    def keys(start, h):
        if fox:
            g = h // 2
            return jnp.concatenate([k_ref[pl.ds(start, tk), g * LANES:(g + 1) * LANES],
                                    k_ref[pl.ds(start, tk), PAIR_W:PAIR_W + LANES]], axis=1)
        return k_ref[pl.ds(start, tk), h * LANES:(h + 1) * LANES]

    def step(j, diagonal):
        start = pl.multiple_of(j * tk, tk)
        s_next = s0_ref[...]
        for h in range(N_HEADS):
            s = s_next
            if h + 1 < N_HEADS:
                s_next = _mm_nt(qop_ref[h + 1], keys(start, h + 1))
            elif not diagonal:
                s0_ref[...] = _mm_nt(qop_ref[0], keys(pl.multiple_of((j + 1) * tk, tk), 0))
            if diagonal:
                r = lax.broadcasted_iota(jnp.int32, s.shape, 0)
                c = lax.broadcasted_iota(jnp.int32, s.shape, 1)
                s = jnp.where(c <= r, s, NEG_BIG)
            m = m_ref[h]
            m_new = jnp.maximum(m, jnp.max(s, axis=-1, keepdims=True))
            p = jnp.concatenate([jnp.exp2(s[:, c * LANES:(c + 1) * LANES] - m_new) for c in range(tk // LANES)],
                                axis=1).astype(MXU_DTYPE)
            acc_ref[h] = (jnp.exp2(m - m_new) * acc_ref[h]
                          + _mm(p, va_ref[pl.ds(start, tk), h * LANES:(h + 1) * LANES]))
            m_ref[h] = m_new

    def body(j, carry):
        step(j, False)
        return carry

    s0_ref[...] = _mm_nt(qop_ref[0], keys(0, 0))
    lax.fori_loop(0, qi, body, 0)
    step(qi, True)
    outs = []
    for h in range(N_HEADS):
        acc = acc_ref[h]
        outs.append(acc / pltpu.roll(acc, HEAD_DIM, axis=1))
    for g in range(N_HEADS // 2):
        o_ref[:, g * LANES:(g + 1) * LANES] = _pack_pair(outs[2 * g], outs[2 * g + 1]).astype(o_ref.dtype)


def _flash(q, k, va, fox):
    B, S, qw = q.shape
    tq = ROW_TILE
    seq = lambda w: pl.BlockSpec((None, S, w), lambda b, i: (b, 0, 0))
    return pl.pallas_call(
        functools.partial(_flash_kernel, fox=fox),
        grid=(B, S // tq),
        in_specs=[pl.BlockSpec((None, tq, qw), lambda b, i: (b, i, 0)), seq(k.shape[-1]), seq(PAD_W)],
        out_specs=pl.BlockSpec((None, tq, PAIR_W), lambda b, i: (b, i, 0)),
        out_shape=jax.ShapeDtypeStruct((B, S, PAIR_W), MXU_DTYPE),
        scratch_shapes=[pltpu.VMEM((N_HEADS, tq, 2 * LANES if fox else LANES), MXU_DTYPE),
                        pltpu.VMEM((N_HEADS, tq, LANES), jnp.float32),
                        pltpu.VMEM((N_HEADS, tq, LANES), jnp.float32),
                        pltpu.VMEM((tq, tq), jnp.float32)],
        compiler_params=pltpu.CompilerParams(
            dimension_semantics=("arbitrary", "arbitrary"), vmem_limit_bytes=VMEM_LIMIT),
        name="fox_attn" if fox else "mla_attn",
    )(q, k, va)


def _dilated_kernel(q_ref, k_ref, va_ref, o_ref, qf_ref, kf_ref, vf_ref, acc_ref, stage_ref):
    S = q_ref.shape[0]
    n = DIL_BLOCK
    tile = ROW_TILE
    quarter = S // 4
    run = n // 4
    keep_stats = _lane((n, LANES)) < HEAD_DIM + HEAD_DIM // 2
    stat0 = jnp.where(_lane((tile, LANES)) < HEAD_DIM + HEAD_DIM // 2, 0.0, NEG_BIG)

    def widen(i, carry):
        rows = pl.ds(pl.multiple_of(i * tile, tile), tile)
        slot = 0
        for src, dst in ((q_ref, qf_ref), (k_ref, kf_ref), (va_ref, vf_ref)):
            for c in range(dst.shape[0]):
                stage_ref[slot] = _f32(src[rows, c * LANES:(c + 1) * LANES])
                for cls in range(4):
                    dst[c, pl.ds(cls * quarter + i * (tile // 4), tile // 4), :] = (
                        stage_ref[slot, pl.ds(cls, tile // 4, stride=4), :])
                slot += 1
        for h in range(N_HEADS):
            acc_ref[h, rows, :] = stat0
        return carry

    lax.fori_loop(0, S // tile, widen, 0)

    ri = lax.broadcasted_iota(jnp.int32, (n, 2 * n), 0)
    ci = lax.broadcasted_iota(jnp.int32, (n, 2 * n), 1)
    diff_class = ri - ci
    diff_runs = (4 * (jnp.bitwise_and(ri, run - 1) - jnp.bitwise_and(ci, 2 * run - 1))
                 + jnp.right_shift(ri, run.bit_length() - 1) - jnp.right_shift(ci, (2 * run).bit_length() - 1))

    def plan(dil, u):
        if dil == 1:
            first = u == 0
            q_runs = [(cls * quarter + run * u, run, 1) for cls in range(4)]
            w_runs = [(cls * quarter + run * jnp.maximum(u - 1, 0), 2 * run, 1) for cls in range(4)]
            return q_runs, w_runs, diff_runs, first
        if dil == 4:
            cls, a = jnp.bitwise_and(u, 3), jnp.right_shift(u, 2)
            base, step, stride = cls * quarter, n, 1
        else:
            rho, a = jnp.bitwise_and(u, 15), jnp.right_shift(u, 4)
            base, step, stride = jnp.bitwise_and(rho, 3) * quarter + jnp.right_shift(rho, 2), 4 * n, 4
        q_runs = [(base + step * a, n, stride)]
        w_runs = [(base + step * jnp.maximum(a - 1, 0), 2 * n, stride)]
        return q_runs, w_runs, diff_class, a == 0

    def rows_of(r):
        start, size, stride = r
        return pl.ds(start, size, stride=stride) if stride > 1 else pl.ds(start, size)

    def load(ref, idx, runs):
        parts = [ref[idx, rows_of(r), :] for r in runs]
        return parts[0] if len(parts) == 1 else jnp.concatenate(parts, axis=0)

    def store(ref, idx, runs, val):
        at = 0
        for r in runs:
            ref[idx, rows_of(r), :] = val[at:at + r[1], :]
            at += r[1]

    for _, dil in DIL_PATTERNS:

        def blocks(t, carry, dil=dil):
            todo = []
            for sub in range(DIL_UNROLL):
                q_runs, w_runs, diff, first = plan(dil, t * DIL_UNROLL + sub)
                dist = diff + jnp.where(first, 0, n)
                valid = (dist >= 0) & (dist <= n)
                q2 = [load(qf_ref, g, q_runs).astype(MXU_DTYPE) for g in range(N_HEADS // 2)]
                k2 = [load(kf_ref, g, w_runs).astype(MXU_DTYPE) for g in range(N_HEADS // 2)]
                v4 = [load(vf_ref, h, w_runs).astype(MXU_DTYPE) for h in range(N_HEADS)]
                old = [load(acc_ref, h, q_runs) for h in range(N_HEADS)]
                m_old = [jnp.broadcast_to(o[:, LANES - 1:LANES], (n, LANES)) for o in old]
                todo.append((q_runs, valid, q2, k2, v4, old, m_old))
            done = []
            for q_runs, valid, q2, k2, v4, old, m_old in todo:
                for h in range(N_HEADS):
                    g, e = divmod(h, 2)
                    keep = (_lane(q2[g].shape) >= HEAD_DIM) if e else (_lane(q2[g].shape) < HEAD_DIM)
                    qh = jnp.where(keep, q2[g], jnp.zeros_like(q2[g]))
                    s = jnp.where(valid, _mm_nt(qh, k2[g]), NEG_BIG)
                    m_new = jnp.maximum(m_old[h], jnp.max(s, axis=-1, keepdims=True))
                    p = jnp.concatenate([jnp.exp2(s[:, c * LANES:(c + 1) * LANES] - m_new) for c in range(2)],
                                        axis=1).astype(MXU_DTYPE)
                    upd = jnp.exp2(m_old[h] - m_new) * old[h] + _mm(p, v4[h])
                    done.append((h, q_runs, jnp.where(keep_stats, upd, m_new)))
            for h, q_runs, val in done:
                store(acc_ref, h, q_runs, val)
            return carry

        lax.fori_loop(0, S // n // DIL_UNROLL, blocks, 0)

    def finish(i, carry):
        rows = pl.ds(pl.multiple_of(i * tile, tile), tile)
        outs = []
        for h in range(N_HEADS):
            for cls in range(4):
                a_h = acc_ref[h, pl.ds(cls * quarter + i * (tile // 4), tile // 4), :]
                stage_ref[h, pl.ds(cls, tile // 4, stride=4), :] = a_h / a_h[:, HEAD_DIM:HEAD_DIM + 1]
            outs.append(stage_ref[h])
        for g in range(N_HEADS // 2):
            o_ref[rows, g * LANES:(g + 1) * LANES] = _pack_pair(outs[2 * g], outs[2 * g + 1]).astype(o_ref.dtype)
        return carry

    lax.fori_loop(0, S // tile, finish, 0)


def _dilated(q, k, va):
    B, S, _ = q.shape
    assert S % (DIL_PATTERNS[-1][1] * 2 * DIL_BLOCK) == 0 and S % ROW_TILE == 0
    seq = lambda w: pl.BlockSpec((None, S, w), lambda b: (b, 0, 0))
    f32 = jnp.float32
    n_chunks = (2 * PAIR_W + PAD_W) // LANES
    return pl.pallas_call(
        _dilated_kernel,
        grid=(B,),
        in_specs=[seq(PAIR_W), seq(PAIR_W), seq(PAD_W)],
        out_specs=seq(PAIR_W),
        out_shape=jax.ShapeDtypeStruct((B, S, PAIR_W), MXU_DTYPE),
        scratch_shapes=[pltpu.VMEM((PAIR_W // LANES, S, LANES), f32), pltpu.VMEM((PAIR_W // LANES, S, LANES), f32),
                        pltpu.VMEM((N_HEADS, S, LANES), f32), pltpu.VMEM((N_HEADS, S, LANES), f32),
                        pltpu.VMEM((n_chunks, ROW_TILE, LANES), f32)],
        compiler_params=pltpu.CompilerParams(dimension_semantics=("arbitrary",), vmem_limit_bytes=VMEM_LIMIT),
        name="dilated",
    )(q, k, va)


def _retention_kernel(q_ref, k_ref, v_ref, sg_ref, gain_ref, o_ref, state_ref, decay_ref, xi_ref, zeta_ref):
    T = q_ref.shape[0]
    lane = _lane((T, LANES))
    low = lane < HEAD_DIM
    row_low = lax.broadcasted_iota(jnp.int32, (LANES, LANES), 0) < HEAD_DIM
    same_head = row_low == (_lane((LANES, LANES)) < HEAD_DIM)
    log_g = [math.log1p(-(2.0 ** (-5.0 - h))) for h in range(N_HEADS)]

    @pl.when((pl.program_id(0) == 0) & (pl.program_id(1) == 0))
    def _():
        pos = _f32(lax.broadcasted_iota(jnp.int32, (T, LANES), 0))
        for c in range(T // LANES):
            rel = pos - _f32(lane) - float(c * LANES)
            for h in range(N_HEADS):
                decay_ref[h, :, c * LANES:(c + 1) * LANES] = jnp.where(
                    rel >= 0, jnp.exp(jnp.maximum(rel, 0.0) * log_g[h]), 0.0)
        for g in range(N_HEADS // 2):
            lg_lane = jnp.where(low, log_g[2 * g], log_g[2 * g + 1])
            xi_ref[g] = jnp.exp((pos + 1.0) * lg_lane)
            zeta_ref[g] = jnp.exp((T - 1.0 - pos) * lg_lane)

    @pl.when(pl.program_id(1) == 0)
    def _():
        state_ref[...] = jnp.zeros_like(state_ref)

    for g in range(N_HEADS // 2):
        cols = slice(g * LANES, (g + 1) * LANES)
        q2 = q_ref[:, cols]
        kf = k_ref[:, cols]
        k2 = kf.astype(MXU_DTYPE)
        v2 = v_ref[:, cols]
        R = state_ref[g]
        cross = _mm(q2, R.astype(MXU_DTYPE)) * xi_ref[g]
        kv = _mm_tn((kf * zeta_ref[g]).astype(MXU_DTYPE), v2)
        g_rows = jnp.where(row_low, math.exp(T * log_g[2 * g]), math.exp(T * log_g[2 * g + 1]))
        state_ref[g] = g_rows * R + jnp.where(same_head, kv, 0.0)
        inner = []
        for e in range(2):
            qm = jnp.where(low if e == 0 else ~low, q2, jnp.zeros_like(q2))
            sc = _mm_nt(qm, k2) * decay_ref[2 * g + e]
            inner.append(_mm(sc.astype(MXU_DTYPE), v2))
        out = jnp.where(low, inner[0], inner[1]) + cross
        mu = jnp.where(low,
                       jnp.sum(jnp.where(low, out, 0.0), axis=-1, keepdims=True),
                       jnp.sum(jnp.where(low, 0.0, out), axis=-1, keepdims=True)) * (1.0 / HEAD_DIM)
        dlt = out - mu
        sq = dlt * dlt
        var = jnp.where(low,
                        jnp.sum(jnp.where(low, sq, 0.0), axis=-1, keepdims=True),
                        jnp.sum(jnp.where(low, 0.0, sq), axis=-1, keepdims=True)) * (1.0 / HEAD_DIM)
        y = dlt * lax.rsqrt(var + GN_EPS) * gain_ref[:, cols]
        o_ref[:, cols] = (sg_ref[:, cols] * y).astype(o_ref.dtype)


def _retention(rq, rk, rv, sg, gain):
    B, S, _ = rq.shape
    T = ROW_TILE
    f32 = jnp.float32
    row = pl.BlockSpec((None, T, PAIR_W), lambda b, i: (b, i, 0))
    return pl.pallas_call(
        _retention_kernel,
        grid=(B, S // T),
        in_specs=[row, row, row, row, pl.BlockSpec((1, PAIR_W), lambda b, i: (0, 0))],
        out_specs=row,
        out_shape=jax.ShapeDtypeStruct((B, S, PAIR_W), MXU_DTYPE),
        scratch_shapes=[pltpu.VMEM((N_HEADS // 2, LANES, LANES), f32), pltpu.VMEM((N_HEADS, T, T), f32),
                        pltpu.VMEM((N_HEADS // 2, T, LANES), f32), pltpu.VMEM((N_HEADS // 2, T, LANES), f32)],
        compiler_params=pltpu.CompilerParams(
            dimension_semantics=("arbitrary", "arbitrary"), vmem_limit_bytes=VMEM_LIMIT),
        name="retention",
    )(rq, rk, rv, sg, gain)


def _merge_kernel(x_ref, gpre_ref, oa_ref, ob_ref, oc_ref, od_ref, wg_ref, wb_ref, wo_ref, gpost_ref, y_ref):
    D = x_ref.shape[1]
    x = x_ref[...]
    h = _rms(x, gpre_ref[...]).astype(MXU_DTYPE)
    merged = None
    for n, o_ref in enumerate((oa_ref, ob_ref, oc_ref, od_ref)):
        gate = 0.5 * jnp.tanh(0.5 * _mm(h, wg_ref[:, n * D:(n + 1) * D])) + 0.5
        term = _mm(o_ref[...], wb_ref[n]) * gate
        merged = term if merged is None else merged + term
    mix = _mm(merged.astype(MXU_DTYPE), wo_ref[...])
    y_ref[...] = x + _rms(mix, gpost_ref[...])


def _merge(x2, gpre, oa, ob, oc, od, wg, wb, wo, gpost):
    N, D = x2.shape
    tm = ROW_TILE
    row = lambda w: pl.BlockSpec((tm, w), lambda i: (i, 0))
    full = lambda a: pl.BlockSpec(a.shape, lambda i: (0,) * a.ndim, pipeline_mode=pl.Buffered(1))
    return pl.pallas_call(
        _merge_kernel,
        grid=(N // tm,),
        in_specs=[row(D), full(gpre), row(PAIR_W), row(PAIR_W), row(PAIR_W), row(PAIR_W),
                  full(wg), full(wb), full(wo), full(gpost)],
        out_specs=row(D),
        out_shape=jax.ShapeDtypeStruct((N, D), jnp.float32),
        compiler_params=pltpu.CompilerParams(dimension_semantics=("arbitrary",), vmem_limit_bytes=VMEM_LIMIT),
        name="merge",
    )(x2, gpre, oa, ob, oc, od, wg, wb, wo, gpost)


def _ffn_kernel(x_ref, gpre_ref, wgate_ref, wup_ref, wdown_ref, gpost_ref, y_ref):
    x = x_ref[...]
    h = _rms(x, gpre_ref[...]).astype(MXU_DTYPE)
    gate = _mm(h, wgate_ref[...])
    act = (gate / (1.0 + jnp.exp(-gate)) * _mm(h, wup_ref[...])).astype(MXU_DTYPE)
    f = _mm(act, wdown_ref[...])
    y_ref[...] = x + _rms(f, gpost_ref[...])


def _ffn(x2, gpre, wgate, wup, wdown, gpost):
    N, D = x2.shape
    tm = ROW_TILE
    row = pl.BlockSpec((tm, D), lambda i: (i, 0))
    full = lambda a: pl.BlockSpec(a.shape, lambda i: (0,) * a.ndim, pipeline_mode=pl.Buffered(1))
    return pl.pallas_call(
        _ffn_kernel,
        grid=(N // tm,),
        in_specs=[row, full(gpre), full(wgate), full(wup), full(wdown), full(gpost)],
        out_specs=row,
        out_shape=jax.ShapeDtypeStruct((N, D), jnp.float32),
        compiler_params=pltpu.CompilerParams(dimension_semantics=("arbitrary",), vmem_limit_bytes=VMEM_LIMIT),
        name="ffn",
    )(x2, gpre, wgate, wup, wdown, gpost)


def _rope_tables(S, half, lo):
    period = 2 * half if lo == 0 else LANES
    inv = ROPE_THETA ** (-jnp.arange(half, dtype=jnp.float32) / half)
    ang = jnp.arange(S, dtype=jnp.float32)[:, None] * inv[None, :]
    cos, sin = jnp.cos(ang), jnp.sin(ang)
    reps = LANES // period
    pad_lo = jnp.zeros((S, lo), jnp.float32)
    pad_hi = jnp.zeros((S, period - lo - 2 * half), jnp.float32)
    zeros = jnp.zeros((S, half), jnp.float32)
    cos_t = jnp.concatenate([pad_lo + 1.0, cos, cos, pad_hi + 1.0], axis=1)
    sin_up = jnp.concatenate([pad_lo, -sin, zeros, pad_hi], axis=1)
    sin_dn = jnp.concatenate([pad_lo, zeros, sin, pad_hi], axis=1)
    return tuple(jnp.tile(t, (1, reps)) for t in (cos_t, sin_up, sin_dn))


def _pad_heads(w, width=HEAD_DIM):
    K = w.shape[0]
    w = w.reshape(K, N_HEADS, width)
    return jnp.pad(w, ((0, 0), (0, 0), (0, LANES - width))).reshape(K, PAD_W)


def _layer_weights(w_in, b_forget, w_uq, w_ukv, w_branch):
    sizes = (PAIR_W, PAIR_W, PAIR_W, N_HEADS, PAIR_W, PAIR_W, PAIR_W, PAIR_W, PAIR_W, PAIR_W, PAIR_W,
             MLA_Q_RANK, MLA_KV_RANK, MLA_ROPE)
    offs = np.cumsum(sizes)[:-1]
    fq, fk, fv, ff, dq, dk, dv, rq, rk, rv, rg, cq, ckv, kr = jnp.split(w_in, offs, axis=1)
    D = w_in.shape[0]
    ffkr = jnp.concatenate([ff, jnp.zeros((D, MLA_NOPE - N_HEADS), w_in.dtype), kr,
                            jnp.zeros((D, LANES - MLA_NOPE - MLA_ROPE), w_in.dtype)], axis=1)
    w_cat = jnp.concatenate([fq, fk, fv, dq, dk, dv, rq, rk, rv, rg, cq, ffkr, ckv], axis=1)
    assert w_cat.shape == (D, IN_CAT_W)
    bf = jnp.pad(b_forget, (0, LANES - N_HEADS)).reshape(1, LANES)
    wuq = _pad_heads(w_uq, MLA_NOPE + MLA_ROPE).astype(MXU_DTYPE)
    ukv = w_ukv.reshape(MLA_KV_RANK, N_HEADS, MLA_NOPE + MLA_V)
    wuk = _pad_heads(ukv[:, :, :MLA_NOPE].reshape(MLA_KV_RANK, -1)).astype(MXU_DTYPE)
    wuv = _pad_heads(ukv[:, :, MLA_NOPE:].reshape(MLA_KV_RANK, -1)).astype(MXU_DTYPE)
    return w_cat, bf, wuq, wuk, wuv, w_branch.astype(MXU_DTYPE)


def kernel(x, w_in, b_forget, ret_gn_gain, mla_q_norm, mla_kv_norm, w_uq, w_ukv, w_gate, w_branch, w_out,
           g_pre_mix, g_post_mix, g_pre_ffn, g_post_ffn, w_ffn_gate, w_ffn_up, w_ffn_down):
    B, S, D = x.shape
    depth = w_in.shape[0]
    assert S % ROW_TILE == 0
    ret_tabs = _rope_tables(S, HEAD_DIM // 2, 0)
    mla_tabs = _rope_tables(S, MLA_ROPE // 2, MLA_NOPE)
    r1 = lambda v: v.reshape(1, -1)
    for l in range(depth):
        w_cat, bf, wuq, wuk, wuv, wb = _layer_weights(w_in[l], b_forget[l], w_uq[l], w_ukv[l], w_branch[l])
        (fq, fk, fva, dq, dk, dva, rq, rk, rv, sg, mq, mk, mva) = _inproj(
            x, r1(g_pre_mix[l]), w_cat, bf, ret_tabs, mla_tabs,
            r1(mla_q_norm[l]), r1(mla_kv_norm[l]), wuq, wuk, wuv)
        o_a = _flash(fq, fk, fva, True)
        o_b = _dilated(dq, dk, dva)
        o_c = _retention(rq, rk, rv, sg, r1(ret_gn_gain[l]))
        o_d = _flash(mq, mk, mva, False)
        flat = lambda a: a.reshape(B * S, a.shape[-1])
        x2 = _merge(flat(x), r1(g_pre_mix[l]), flat(o_a), flat(o_b), flat(o_c), flat(o_d),
                    w_gate[l].astype(MXU_DTYPE), wb, w_out[l].astype(MXU_DTYPE), r1(g_post_mix[l]))
        x2 = _ffn(x2, r1(g_pre_ffn[l]), w_ffn_gate[l].astype(MXU_DTYPE), w_ffn_up[l].astype(MXU_DTYPE),
                  w_ffn_down[l].astype(MXU_DTYPE), r1(g_post_ffn[l]))
        x = x2.reshape(B, S, D)
    return x
```

```python
import functools
import math

import jax
import jax.numpy as jnp
import numpy as np
from jax import lax
from jax.experimental import pallas as pl
from jax.experimental.pallas import tpu as pltpu

HEAD_DIM = 64
N_HEADS = 4
PAIR_W = N_HEADS * HEAD_DIM
LANES = 128
PAD_W = N_HEADS * LANES
DIL_PATTERNS = ((128, 1), (512, 4), (2048, 16))
DIL_BLOCK = 128
DIL_UNROLL = 8
RET_CHUNK = 128
MLA_Q_RANK = 256
MLA_KV_RANK = 128
MLA_NOPE = 64
MLA_ROPE = 32
MLA_V = 64
ROPE_THETA = 10000.0
RMS_EPS = 1e-6
GN_EPS = 1e-5
NEG_BIG = -1e30
LOG2E = math.log2(math.e)

MXU_DTYPE = jnp.bfloat16
ROW_TILE = 512
VMEM_LIMIT = 56 * 1024 * 1024

_C_FQ, _C_FK, _C_FV = 0, 256, 512
_C_DQ, _C_DK, _C_DV = 768, 1024, 1280
_C_RQ, _C_RK, _C_RV, _C_RG = 1536, 1792, 2048, 2304
_C_CQ, _C_TAIL = 2560, 2816
IN_CAT_W = 3072
IN_FF_COL = 3 * PAIR_W
IN_KR_COL = 2948


def _f32(x):
    return x.astype(jnp.float32)


def _rms(x, g):
    ms = jnp.mean(x * x, axis=-1, keepdims=True)
    return x * lax.rsqrt(ms + RMS_EPS) * g


def _mm(a, b):
    return jnp.dot(a, b, preferred_element_type=jnp.float32)


def _mm_nt(a, b):
    return lax.dot_general(a, b, (((1,), (1,)), ((), ())), preferred_element_type=jnp.float32)


def _mm_tn(a, b):
    return lax.dot_general(a, b, (((0,), (0,)), ((), ())), preferred_element_type=jnp.float32)


def _lane(shape):
    return lax.broadcasted_iota(jnp.int32, shape, len(shape) - 1)


def _rope_chunks(a, cos, sin_a, sin_b, half):
    outs = []
    for c in range(a.shape[1] // LANES):
        t = a[:, c * LANES:(c + 1) * LANES]
        up = pltpu.roll(t, LANES - half, axis=1)
        dn = pltpu.roll(t, half, axis=1)
        outs.append(t * cos + up * sin_a + dn * sin_b)
    return outs[0] if len(outs) == 1 else jnp.concatenate(outs, axis=1)


def _inproj_kernel(x_ref, g_ref, w_ref, bf_ref, rcos_ref, rsa_ref, rsb_ref, mcos_ref, msa_ref, msb_ref,
                   qn_ref, kvn_ref, wuq_ref, wuk_ref, wuv_ref,
                   fq_ref, fk_ref, fva_ref,
                   dq_ref, dk_ref, dva_ref,
                   rq_ref, rk_ref, rv_ref, sg_ref,
                   mq_ref, mk_ref, mva_ref,
                   carry_ref, wb_ref, *, mla_scale):
    tm = x_ref.shape[0]

    @pl.when((pl.program_id(0) == 0) & (pl.program_id(1) == 0))
    def _():
        n_ff = N_HEADS
        for r in range(0, IN_FF_COL, PAIR_W):
            wb_ref[r:r + PAIR_W, :] = w_ref[r:r + PAIR_W, :].astype(wb_ref.dtype)
        for r in range(IN_FF_COL, _C_TAIL + LANES, LANES):
            wb_ref[r:r + LANES, :] = w_ref[r + n_ff:r + n_ff + LANES, :].astype(wb_ref.dtype)
        blk = _C_TAIL + LANES
        wb_ref[blk:blk + LANES, :] = jnp.zeros((LANES, wb_ref.shape[1]), wb_ref.dtype)
        head = w_ref[IN_FF_COL:IN_FF_COL + 2 * n_ff, :]
        is_ff = lax.broadcasted_iota(jnp.int32, head.shape, 0) < n_ff
        wb_ref[blk:blk + 2 * n_ff, :] = jnp.where(is_ff, head, 0.0).astype(wb_ref.dtype)
        wb_ref[blk + MLA_NOPE:blk + MLA_NOPE + MLA_ROPE, :] = (
            w_ref[IN_KR_COL:IN_KR_COL + MLA_ROPE, :].astype(wb_ref.dtype))

    @pl.when(pl.program_id(1) == 0)
    def _():
        carry_ref[...] = jnp.zeros_like(carry_ref)

    h = _rms(x_ref[...], g_ref[...]).astype(MXU_DTYPE)

    def proj(lo, width):
        return _mm_nt(h, wb_ref[lo:lo + width, :])

    ones_up = jnp.where(_lane((1, PAD_W)) % LANES >= HEAD_DIM, 1.0, 0.0)
    mcos, msa, msb = mcos_ref[...], msa_ref[...], msb_ref[...]
    rcos, rsa, rsb = rcos_ref[...], rsa_ref[...], rsb_ref[...]

    def with_ones(v):
        out = []
        for hd in range(N_HEADS):
            chunk = v[:, (hd // 2) * LANES:(hd // 2 + 1) * LANES]
            if hd % 2:
                chunk = pltpu.roll(chunk, HEAD_DIM, axis=1)
            out.append(jnp.where(_lane(chunk.shape) < HEAD_DIM, chunk, 1.0))
        return jnp.concatenate(out, axis=1)


    tail = proj(_C_TAIL, 2 * LANES)
    cq_raw = proj(_C_CQ, MLA_Q_RANK)
    ffkr = tail[:, LANES:2 * LANES]
    ff = ffkr + bf_ref[...]
    log_f = jnp.minimum(ff, 0.0) - jnp.log1p(jnp.exp(-jnp.abs(ff)))
    rows = lax.broadcasted_iota(jnp.int32, (tm, LANES), 0)
    cs = log_f
    step = 1
    while step < tm:
        cs = cs + jnp.where(rows >= step, pltpu.roll(cs, step, axis=0), 0.0)
        step *= 2
    cs = cs + carry_ref[...]
    carry_ref[...] = cs[tm - 1:tm, :]
    cs2 = cs * LOG2E
    hi = _f32(cs2.astype(MXU_DTYPE))
    rem = cs2 - hi
    mid = _f32(rem.astype(MXU_DTYPE))
    lo = rem - mid
    lane = _lane(cs2.shape)
    split = jnp.where(lane < N_HEADS, hi,
                      jnp.where(lane < 2 * N_HEADS, pltpu.roll(mid, N_HEADS, axis=1),
                                jnp.where(lane < 3 * N_HEADS, pltpu.roll(lo, 2 * N_HEADS, axis=1), 0.0)))
    fk_ref[:, PAIR_W:PAIR_W + LANES] = (-split).astype(fk_ref.dtype)

    rq_ref[...] = _rope_chunks(proj(_C_RQ, PAIR_W), rcos, rsa, rsb, HEAD_DIM // 2).astype(rq_ref.dtype)
    rk_ref[...] = _rope_chunks(proj(_C_RK, PAIR_W), rcos, rsa, rsb, HEAD_DIM // 2) * HEAD_DIM ** -0.5
    rg = proj(_C_RG, PAIR_W)
    sg_ref[...] = rg / (1.0 + jnp.exp(-rg))

    ckv = _rms(tail[:, 0:LANES], kvn_ref[...]).astype(MXU_DTYPE)
    rope_lanes = (_lane(ffkr.shape) >= MLA_NOPE) & (_lane(ffkr.shape) < MLA_NOPE + MLA_ROPE)
    kr = jnp.where(rope_lanes, _rope_chunks(ffkr, mcos, msa, msb, MLA_ROPE // 2), 0.0)
    mk_ref[...] = (_mm(ckv, wuk_ref[...]) + jnp.concatenate([kr] * N_HEADS, axis=1)).astype(mk_ref.dtype)
    mva_ref[...] = (_mm(ckv, wuv_ref[...]) + ones_up).astype(mva_ref.dtype)
    cq = _rms(cq_raw, qn_ref[...]).astype(MXU_DTYPE)
    q = _rope_chunks(_mm(cq, wuq_ref[...]), mcos, msa, msb, MLA_ROPE // 2)
    mq_ref[...] = (q * mla_scale).astype(mq_ref.dtype)

    fva_ref[...] = with_ones(proj(_C_FV, PAIR_W)).astype(fva_ref.dtype)
    dva_ref[...] = with_ones(proj(_C_DV, PAIR_W)).astype(dva_ref.dtype)
    fq_ref[...] = (proj(_C_FQ, PAIR_W) * (HEAD_DIM ** -0.5 * LOG2E)).astype(fq_ref.dtype)
    dq_ref[...] = (proj(_C_DQ, PAIR_W) * (HEAD_DIM ** -0.5 * LOG2E)).astype(dq_ref.dtype)

    fk_ref[:, 0:PAIR_W] = proj(_C_FK, PAIR_W).astype(fk_ref.dtype)
    dk_ref[...] = proj(_C_DK, PAIR_W).astype(dk_ref.dtype)
    rv_ref[...] = proj(_C_RV, PAIR_W).astype(rv_ref.dtype)


def _layer(a, l, n_grid):
    zeros = (0,) * (a.ndim - 1)
    index = (lambda b, i: (l,) + zeros) if n_grid == 2 else (lambda i: (l,) + zeros)
    return pl.BlockSpec((None,) + a.shape[1:], index, pipeline_mode=pl.Buffered(1))


def _inproj(x, g, w_in, bf, ret_tabs, mla_tabs, qn, kvn, wuq, wuk, wuv):
    B, S, D = x.shape
    tm = ROW_TILE
    ns = S // tm
    row = lambda w: pl.BlockSpec((None, tm, w), lambda b, i: (b, i, 0))
    full = lambda a: pl.BlockSpec(a.shape, lambda b, i: (0,) * a.ndim, pipeline_mode=pl.Buffered(1))
    tab = pl.BlockSpec((tm, LANES), lambda b, i: (i, 0))
    bf16, f32 = MXU_DTYPE, jnp.float32
    out_defs = [
        (PAIR_W, bf16), (PAIR_W + LANES, bf16), (PAD_W, bf16),
        (PAIR_W, bf16), (PAIR_W, bf16), (PAD_W, bf16),
        (PAIR_W, bf16), (PAIR_W, f32), (PAIR_W, bf16), (PAIR_W, f32),
        (PAD_W, bf16), (PAD_W, bf16), (PAD_W, bf16),
    ]
    out_shape, out_specs = [], []
    for d in out_defs:
        out_shape.append(jax.ShapeDtypeStruct((B, S, d[0]), d[1]))
        out_specs.append(row(d[0]))
    tabs = list(ret_tabs) + list(mla_tabs)
    tail = [qn, kvn, wuq, wuk, wuv]
    return pl.pallas_call(
        functools.partial(_inproj_kernel, mla_scale=(MLA_NOPE + MLA_ROPE) ** -0.5 * LOG2E),
        grid=(B, ns),
        in_specs=[row(D), full(g), full(w_in), full(bf)] + [tab] * 6 + [full(a) for a in tail],
        out_specs=out_specs,
        out_shape=out_shape,
        scratch_shapes=[pltpu.VMEM((1, LANES), f32), pltpu.VMEM((IN_CAT_W, D), bf16)],
        compiler_params=pltpu.CompilerParams(
            dimension_semantics=("arbitrary", "arbitrary"), vmem_limit_bytes=VMEM_LIMIT),
        name="inproj",
    )(x, g, w_in, bf, *tabs, *tail)


def _head_q(q_ref, h, pair_q):
    if pair_q:
        g, e = divmod(h, 2)
        q2 = q_ref[:, g * LANES:(g + 1) * LANES]
        keep = (_lane(q2.shape) >= HEAD_DIM) if e else (_lane(q2.shape) < HEAD_DIM)
        return jnp.where(keep, q2, jnp.zeros_like(q2)), g
    return q_ref[:, h * LANES:(h + 1) * LANES], h


def _pack_pair(o_even, o_odd):
    return jnp.where(_lane(o_even.shape) < HEAD_DIM, o_even, pltpu.roll(o_odd, HEAD_DIM, axis=1))


def _flash_kernel(q_ref, k_ref, va_ref, o_ref, qop_ref, m_ref, acc_ref, s0_ref, *, fox):
    tq = q_ref.shape[0]
    tk = tq
    qi = pl.program_id(1)
    for h in range(N_HEADS):
        if fox:
            qm, _ = _head_q(q_ref, h, True)
            lane = _lane(qm.shape)
            ones = jnp.where((lane % N_HEADS == h) & (lane < 3 * N_HEADS), 1.0, 0.0).astype(qm.dtype)
            qop_ref[h] = jnp.concatenate([qm, ones], axis=1)
        else:
            qop_ref[h] = q_ref[:, h * LANES:(h + 1) * LANES]
    m_ref[...] = jnp.full(m_ref.shape, NEG_BIG, jnp.float32)
    acc_ref[...] = jnp.zeros(acc_ref.shape, jnp.float32)

    def keys(start, h):
        if fox:
            g = h // 2
            return jnp.concatenate([k_ref[pl.ds(start, tk), g * LANES:(g + 1) * LANES],
                                    k_ref[pl.ds(start, tk), PAIR_W:PAIR_W + LANES]], axis=1)
        return k_ref[pl.ds(start, tk), h * LANES:(h + 1) * LANES]

    def step(j, diagonal):
        start = pl.multiple_of(j * tk, tk)
        s_next = s0_ref[...]
        for h in range(N_HEADS):
            s = s_next
            if h + 1 < N_HEADS:
                s_next = _mm_nt(qop_ref[h + 1], keys(start, h + 1))
            elif not diagonal:
                s0_ref[...] = _mm_nt(qop_ref[0], keys(pl.multiple_of((j + 1) * tk, tk), 0))
            if diagonal:
                r = lax.broadcasted_iota(jnp.int32, s.shape, 0)
                c = lax.broadcasted_iota(jnp.int32, s.shape, 1)
                s = jnp.where(c <= r, s, NEG_BIG)
            m = m_ref[h]
            m_new = jnp.maximum(m, jnp.max(s, axis=-1, keepdims=True))
            p = jnp.concatenate([jnp.exp2(s[:, c * LANES:(c + 1) * LANES] - m_new) for c in range(tk // LANES)],
                                axis=1).astype(MXU_DTYPE)
            acc_ref[h] = (jnp.exp2(m - m_new) * acc_ref[h]
                          + _mm(p, va_ref[pl.ds(start, tk), h * LANES:(h + 1) * LANES]))
            m_ref[h] = m_new

    def body(j, carry):
        step(j, False)
        return carry

    s0_ref[...] = _mm_nt(qop_ref[0], keys(0, 0))
    lax.fori_loop(0, qi, body, 0)
    step(qi, True)
    outs = []
    for h in range(N_HEADS):
        acc = acc_ref[h]
        outs.append(acc / pltpu.roll(acc, HEAD_DIM, axis=1))
    for g in range(N_HEADS // 2):
        o_ref[:, g * LANES:(g + 1) * LANES] = _pack_pair(outs[2 * g], outs[2 * g + 1]).astype(o_ref.dtype)


def _flash(q, k, va, fox):
    B, S, qw = q.shape
    tq = ROW_TILE
    seq = lambda w: pl.BlockSpec((None, S, w), lambda b, i: (b, 0, 0))
    return pl.pallas_call(
        functools.partial(_flash_kernel, fox=fox),
        grid=(B, S // tq),
        in_specs=[pl.BlockSpec((None, tq, qw), lambda b, i: (b, i, 0)), seq(k.shape[-1]), seq(PAD_W)],
        out_specs=pl.BlockSpec((None, tq, PAIR_W), lambda b, i: (b, i, 0)),
        out_shape=jax.ShapeDtypeStruct((B, S, PAIR_W), MXU_DTYPE),
        scratch_shapes=[pltpu.VMEM((N_HEADS, tq, 2 * LANES if fox else LANES), MXU_DTYPE),
                        pltpu.VMEM((N_HEADS, tq, LANES), jnp.float32),
                        pltpu.VMEM((N_HEADS, tq, LANES), jnp.float32),
                        pltpu.VMEM((tq, tq), jnp.float32)],
        compiler_params=pltpu.CompilerParams(
            dimension_semantics=("arbitrary", "arbitrary"), vmem_limit_bytes=VMEM_LIMIT),
        name="fox_attn" if fox else "mla_attn",
    )(q, k, va)


def _dilated_kernel(q_ref, k_ref, va_ref, o_ref, qf_ref, kf_ref, vf_ref, acc_ref, stage_ref):
    S = q_ref.shape[0]
    n = DIL_BLOCK
    tile = ROW_TILE
    quarter = S // 4
    run = n // 4
    keep_stats = _lane((n, LANES)) < HEAD_DIM + HEAD_DIM // 2
    stat0 = jnp.where(_lane((tile, LANES)) < HEAD_DIM + HEAD_DIM // 2, 0.0, NEG_BIG)

    def widen(i, carry):
        rows = pl.ds(pl.multiple_of(i * tile, tile), tile)
        slot = 0
        for src, dst in ((q_ref, qf_ref), (k_ref, kf_ref), (va_ref, vf_ref)):
            for c in range(dst.shape[0]):
                stage_ref[slot] = _f32(src[rows, c * LANES:(c + 1) * LANES])
                for cls in range(4):
                    dst[c, pl.ds(cls * quarter + i * (tile // 4), tile // 4), :] = (
                        stage_ref[slot, pl.ds(cls, tile // 4, stride=4), :])
                slot += 1
        for h in range(N_HEADS):
            acc_ref[h, rows, :] = stat0
        return carry

    lax.fori_loop(0, S // tile, widen, 0)

    ri = lax.broadcasted_iota(jnp.int32, (n, 2 * n), 0)
    ci = lax.broadcasted_iota(jnp.int32, (n, 2 * n), 1)
    diff_class = ri - ci
    diff_runs = (4 * (jnp.bitwise_and(ri, run - 1) - jnp.bitwise_and(ci, 2 * run - 1))
                 + jnp.right_shift(ri, run.bit_length() - 1) - jnp.right_shift(ci, (2 * run).bit_length() - 1))

    def plan(dil, u):
        if dil == 1:
            first = u == 0
            q_runs = [(cls * quarter + run * u, run, 1) for cls in range(4)]
            w_runs = [(cls * quarter + run * jnp.maximum(u - 1, 0), 2 * run, 1) for cls in range(4)]
            return q_runs, w_runs, diff_runs, first
        if dil == 4:
            cls, a = jnp.bitwise_and(u, 3), jnp.right_shift(u, 2)
            base, step, stride = cls * quarter, n, 1
        else:
            rho, a = jnp.bitwise_and(u, 15), jnp.right_shift(u, 4)
            base, step, stride = jnp.bitwise_and(rho, 3) * quarter + jnp.right_shift(rho, 2), 4 * n, 4
        q_runs = [(base + step * a, n, stride)]
        w_runs = [(base + step * jnp.maximum(a - 1, 0), 2 * n, stride)]
        return q_runs, w_runs, diff_class, a == 0

    def rows_of(r):
        start, size, stride = r
        return pl.ds(start, size, stride=stride) if stride > 1 else pl.ds(start, size)

    def load(ref, idx, runs):
        parts = [ref[idx, rows_of(r), :] for r in runs]
        return parts[0] if len(parts) == 1 else jnp.concatenate(parts, axis=0)

    def store(ref, idx, runs, val):
        at = 0
        for r in runs:
            ref[idx, rows_of(r), :] = val[at:at + r[1], :]
            at += r[1]

    for _, dil in DIL_PATTERNS:

        def blocks(t, carry, dil=dil):
            todo = []
            for sub in range(DIL_UNROLL):
                q_runs, w_runs, diff, first = plan(dil, t * DIL_UNROLL + sub)
                dist = diff + jnp.where(first, 0, n)
                valid = (dist >= 0) & (dist <= n)
                q2 = [load(qf_ref, g, q_runs).astype(MXU_DTYPE) for g in range(N_HEADS // 2)]
                k2 = [load(kf_ref, g, w_runs).astype(MXU_DTYPE) for g in range(N_HEADS // 2)]
                v4 = [load(vf_ref, h, w_runs).astype(MXU_DTYPE) for h in range(N_HEADS)]
                old = [load(acc_ref, h, q_runs) for h in range(N_HEADS)]
                m_old = [jnp.broadcast_to(o[:, LANES - 1:LANES], (n, LANES)) for o in old]
                todo.append((q_runs, valid, q2, k2, v4, old, m_old))
            done = []
            for q_runs, valid, q2, k2, v4, old, m_old in todo:
                for h in range(N_HEADS):
                    g, e = divmod(h, 2)
                    keep = (_lane(q2[g].shape) >= HEAD_DIM) if e else (_lane(q2[g].shape) < HEAD_DIM)
                    qh = jnp.where(keep, q2[g], jnp.zeros_like(q2[g]))
                    s = jnp.where(valid, _mm_nt(qh, k2[g]), NEG_BIG)
                    m_new = jnp.maximum(m_old[h], jnp.max(s, axis=-1, keepdims=True))
                    p = jnp.concatenate([jnp.exp2(s[:, c * LANES:(c + 1) * LANES] - m_new) for c in range(2)],
                                        axis=1).astype(MXU_DTYPE)
                    upd = jnp.exp2(m_old[h] - m_new) * old[h] + _mm(p, v4[h])
                    done.append((h, q_runs, jnp.where(keep_stats, upd, m_new)))
            for h, q_runs, val in done:
                store(acc_ref, h, q_runs, val)
            return carry

        lax.fori_loop(0, S // n // DIL_UNROLL, blocks, 0)

    def finish(i, carry):
        rows = pl.ds(pl.multiple_of(i * tile, tile), tile)
        outs = []
        for h in range(N_HEADS):
            for cls in range(4):
                a_h = acc_ref[h, pl.ds(cls * quarter + i * (tile // 4), tile // 4), :]
                stage_ref[h, pl.ds(cls, tile // 4, stride=4), :] = a_h / a_h[:, HEAD_DIM:HEAD_DIM + 1]
            outs.append(stage_ref[h])
        for g in range(N_HEADS // 2):
            o_ref[rows, g * LANES:(g + 1) * LANES] = _pack_pair(outs[2 * g], outs[2 * g + 1]).astype(o_ref.dtype)
        return carry

    lax.fori_loop(0, S // tile, finish, 0)


def _dilated(q, k, va):
    B, S, _ = q.shape
    assert S % (DIL_PATTERNS[-1][1] * 2 * DIL_BLOCK) == 0 and S % ROW_TILE == 0
    seq = lambda w: pl.BlockSpec((None, S, w), lambda b: (b, 0, 0))
    f32 = jnp.float32
    n_chunks = (2 * PAIR_W + PAD_W) // LANES
    return pl.pallas_call(
        _dilated_kernel,
        grid=(B,),
        in_specs=[seq(PAIR_W), seq(PAIR_W), seq(PAD_W)],
        out_specs=seq(PAIR_W),
        out_shape=jax.ShapeDtypeStruct((B, S, PAIR_W), MXU_DTYPE),
        scratch_shapes=[pltpu.VMEM((PAIR_W // LANES, S, LANES), f32), pltpu.VMEM((PAIR_W // LANES, S, LANES), f32),
                        pltpu.VMEM((N_HEADS, S, LANES), f32), pltpu.VMEM((N_HEADS, S, LANES), f32),
                        pltpu.VMEM((n_chunks, ROW_TILE, LANES), f32)],
        compiler_params=pltpu.CompilerParams(dimension_semantics=("arbitrary",), vmem_limit_bytes=VMEM_LIMIT),
        name="dilated",
    )(q, k, va)


def _retention_kernel(q_ref, k_ref, v_ref, sg_ref, gain_ref, o_ref, state_ref, decay_ref, xi_ref, zeta_ref):
    T = q_ref.shape[0]
    lane = _lane((T, LANES))
    low = lane < HEAD_DIM
    row_low = lax.broadcasted_iota(jnp.int32, (LANES, LANES), 0) < HEAD_DIM
    same_head = row_low == (_lane((LANES, LANES)) < HEAD_DIM)
    log_g = [math.log1p(-(2.0 ** (-5.0 - h))) for h in range(N_HEADS)]

    @pl.when((pl.program_id(0) == 0) & (pl.program_id(1) == 0))
    def _():
        pos = _f32(lax.broadcasted_iota(jnp.int32, (T, LANES), 0))
        for c in range(T // LANES):
            rel = pos - _f32(lane) - float(c * LANES)
            for h in range(N_HEADS):
                decay_ref[h, :, c * LANES:(c + 1) * LANES] = jnp.where(
                    rel >= 0, jnp.exp(jnp.maximum(rel, 0.0) * log_g[h]), 0.0)
        for g in range(N_HEADS // 2):
            lg_lane = jnp.where(low, log_g[2 * g], log_g[2 * g + 1])
            xi_ref[g] = jnp.exp((pos + 1.0) * lg_lane)
            zeta_ref[g] = jnp.exp((T - 1.0 - pos) * lg_lane)

    @pl.when(pl.program_id(1) == 0)
    def _():
        state_ref[...] = jnp.zeros_like(state_ref)

    for g in range(N_HEADS // 2):
        cols = slice(g * LANES, (g + 1) * LANES)
        q2 = q_ref[:, cols]
        kf = k_ref[:, cols]
        k2 = kf.astype(MXU_DTYPE)
        v2 = v_ref[:, cols]
        R = state_ref[g]
        cross = _mm(q2, R.astype(MXU_DTYPE)) * xi_ref[g]
        kv = _mm_tn((kf * zeta_ref[g]).astype(MXU_DTYPE), v2)
        g_rows = jnp.where(row_low, math.exp(T * log_g[2 * g]), math.exp(T * log_g[2 * g + 1]))
        state_ref[g] = g_rows * R + jnp.where(same_head, kv, 0.0)
        inner = []
        for e in range(2):
            qm = jnp.where(low if e == 0 else ~low, q2, jnp.zeros_like(q2))
            sc = _mm_nt(qm, k2) * decay_ref[2 * g + e]
            inner.append(_mm(sc.astype(MXU_DTYPE), v2))
        out = jnp.where(low, inner[0], inner[1]) + cross
        mu = jnp.where(low,
                       jnp.sum(jnp.where(low, out, 0.0), axis=-1, keepdims=True),
                       jnp.sum(jnp.where(low, 0.0, out), axis=-1, keepdims=True)) * (1.0 / HEAD_DIM)
        dlt = out - mu
        sq = dlt * dlt
        var = jnp.where(low,
                        jnp.sum(jnp.where(low, sq, 0.0), axis=-1, keepdims=True),
                        jnp.sum(jnp.where(low, 0.0, sq), axis=-1, keepdims=True)) * (1.0 / HEAD_DIM)
        y = dlt * lax.rsqrt(var + GN_EPS) * gain_ref[:, cols]
        o_ref[:, cols] = (sg_ref[:, cols] * y).astype(o_ref.dtype)


def _retention(rq, rk, rv, sg, gain):
    B, S, _ = rq.shape
    T = ROW_TILE
    f32 = jnp.float32
    row = pl.BlockSpec((None, T, PAIR_W), lambda b, i: (b, i, 0))
    return pl.pallas_call(
        _retention_kernel,
        grid=(B, S // T),
        in_specs=[row, row, row, row, pl.BlockSpec((1, PAIR_W), lambda b, i: (0, 0))],
        out_specs=row,
        out_shape=jax.ShapeDtypeStruct((B, S, PAIR_W), MXU_DTYPE),
        scratch_shapes=[pltpu.VMEM((N_HEADS // 2, LANES, LANES), f32), pltpu.VMEM((N_HEADS, T, T), f32),
                        pltpu.VMEM((N_HEADS // 2, T, LANES), f32), pltpu.VMEM((N_HEADS // 2, T, LANES), f32)],
        compiler_params=pltpu.CompilerParams(
            dimension_semantics=("arbitrary", "arbitrary"), vmem_limit_bytes=VMEM_LIMIT),
        name="retention",
    )(rq, rk, rv, sg, gain)


def _merge_kernel(x_ref, gpre_ref, oa_ref, ob_ref, oc_ref, od_ref, wg_ref, wb_ref, wo_ref, gpost_ref, y_ref):
    D = x_ref.shape[1]
    x = x_ref[...]
    h = _rms(x, gpre_ref[...]).astype(MXU_DTYPE)
    twice_merged = None
    for n, o_ref in enumerate((oa_ref, ob_ref, oc_ref, od_ref)):
        t = jnp.tanh(_mm(h, wg_ref[:, n * D:(n + 1) * D]))
        proj = _mm(o_ref[...], wb_ref[n])
        term = proj * t + proj
        twice_merged = term if twice_merged is None else twice_merged + term
    mix = _mm(twice_merged.astype(MXU_DTYPE), wo_ref[...])
    y_ref[...] = x + _rms(mix, gpost_ref[...])


def _merge(x2, l, gpre, oa, ob, oc, od, wg_half, wb, wo_half, gpost):
    N, D = x2.shape
    tm = ROW_TILE
    row = lambda w: pl.BlockSpec((tm, w), lambda i: (i, 0))
    full = lambda a: pl.BlockSpec(a.shape, lambda i: (0,) * a.ndim, pipeline_mode=pl.Buffered(1))
    wg, wo = wg_half, wo_half
    return pl.pallas_call(
        _merge_kernel,
        grid=(N // tm,),
        in_specs=[row(D), full(gpre), row(PAIR_W), row(PAIR_W), row(PAIR_W), row(PAIR_W),
                  _layer(wg, l, 1), _layer(wb, l, 1), _layer(wo, l, 1), full(gpost)],
        out_specs=row(D),
        out_shape=jax.ShapeDtypeStruct((N, D), jnp.float32),
        compiler_params=pltpu.CompilerParams(dimension_semantics=("arbitrary",), vmem_limit_bytes=VMEM_LIMIT),
        name="merge",
    )(x2, gpre, oa, ob, oc, od, wg, wb, wo, gpost)


def _ffn_kernel(x_ref, gpre_ref, wgate_ref, wup_ref, wdown_ref, gpost_ref, y_ref):
    x = x_ref[...]
    h = _rms(x, gpre_ref[...]).astype(MXU_DTYPE)
    gate = _mm(h, wgate_ref[...])
    act = (gate / (1.0 + jnp.exp(-gate)) * _mm(h, wup_ref[...])).astype(MXU_DTYPE)
    f = _mm(act, wdown_ref[...])
    y_ref[...] = x + _rms(f, gpost_ref[...])


def _ffn(x2, l, gpre, wgate, wup, wdown, gpost):
    N, D = x2.shape
    tm = ROW_TILE
    row = pl.BlockSpec((tm, D), lambda i: (i, 0))
    full = lambda a: pl.BlockSpec(a.shape, lambda i: (0,) * a.ndim, pipeline_mode=pl.Buffered(1))
    return pl.pallas_call(
        _ffn_kernel,
        grid=(N // tm,),
        in_specs=[row, full(gpre), _layer(wgate, l, 1), _layer(wup, l, 1), _layer(wdown, l, 1), full(gpost)],
        out_specs=row,
        out_shape=jax.ShapeDtypeStruct((N, D), jnp.float32),
        compiler_params=pltpu.CompilerParams(dimension_semantics=("arbitrary",), vmem_limit_bytes=VMEM_LIMIT),
        name="ffn",
    )(x2, gpre, wgate, wup, wdown, gpost)


def _rope_tables(S, half, lo):
    period = 2 * half if lo == 0 else LANES
    inv = ROPE_THETA ** (-jnp.arange(half, dtype=jnp.float32) / half)
    ang = jnp.arange(S, dtype=jnp.float32)[:, None] * inv[None, :]
    cos, sin = jnp.cos(ang), jnp.sin(ang)
    reps = LANES // period
    pad_lo = jnp.zeros((S, lo), jnp.float32)
    pad_hi = jnp.zeros((S, period - lo - 2 * half), jnp.float32)
    zeros = jnp.zeros((S, half), jnp.float32)
    cos_t = jnp.concatenate([pad_lo + 1.0, cos, cos, pad_hi + 1.0], axis=1)
    sin_up = jnp.concatenate([pad_lo, -sin, zeros, pad_hi], axis=1)
    sin_dn = jnp.concatenate([pad_lo, zeros, sin, pad_hi], axis=1)
    return tuple(jnp.tile(t, (1, reps)) for t in (cos_t, sin_up, sin_dn))


def _pad_heads(w, width=HEAD_DIM):
    K = w.shape[0]
    w = w.reshape(K, N_HEADS, width)
    return jnp.pad(w, ((0, 0), (0, 0), (0, LANES - width))).reshape(K, PAD_W)


def _layer_weights(b_forget, w_uq, w_ukv):
    bf = jnp.pad(b_forget, (0, LANES - N_HEADS)).reshape(1, LANES)
    wuq = _pad_heads(w_uq, MLA_NOPE + MLA_ROPE).astype(MXU_DTYPE)
    ukv = w_ukv.reshape(MLA_KV_RANK, N_HEADS, MLA_NOPE + MLA_V)
    wuk = _pad_heads(ukv[:, :, :MLA_NOPE].reshape(MLA_KV_RANK, -1)).astype(MXU_DTYPE)
    wuv = _pad_heads(ukv[:, :, MLA_NOPE:].reshape(MLA_KV_RANK, -1)).astype(MXU_DTYPE)
    return bf, wuq, wuk, wuv


def kernel(x, w_in, b_forget, ret_gn_gain, mla_q_norm, mla_kv_norm, w_uq, w_ukv, w_gate, w_branch, w_out,
           g_pre_mix, g_post_mix, g_pre_ffn, g_post_ffn, w_ffn_gate, w_ffn_up, w_ffn_down):
    B, S, D = x.shape
    depth = w_in.shape[0]
    assert S % ROW_TILE == 0
    ret_tabs = _rope_tables(S, HEAD_DIM // 2, 0)
    mla_tabs = _rope_tables(S, MLA_ROPE // 2, MLA_NOPE)
    r1 = lambda v: v.reshape(1, -1)
    assert w_in.shape[2] == IN_KR_COL + MLA_ROPE
    wg_half = (0.5 * w_gate).astype(MXU_DTYPE)
    wo_half = (0.5 * w_out).astype(MXU_DTYPE)
    wb = w_branch.astype(MXU_DTYPE)
    wf_gate, wf_up, wf_down = (w.astype(MXU_DTYPE) for w in (w_ffn_gate, w_ffn_up, w_ffn_down))
    for l in range(depth):
        bf, wuq, wuk, wuv = _layer_weights(b_forget[l], w_uq[l], w_ukv[l])
        (fq, fk, fva, dq, dk, dva, rq, rk, rv, sg, mq, mk, mva) = _inproj(
            x, r1(g_pre_mix[l]), w_in[l].T, bf, ret_tabs, mla_tabs,
            r1(mla_q_norm[l]), r1(mla_kv_norm[l]), wuq, wuk, wuv)
        o_a = _flash(fq, fk, fva, True)
        o_b = _dilated(dq, dk, dva)
        o_c = _retention(rq, rk, rv, sg, r1(ret_gn_gain[l]))
        o_d = _flash(mq, mk, mva, False)
        flat = lambda a: a.reshape(B * S, a.shape[-1])
        x2 = _merge(flat(x), l, r1(g_pre_mix[l]), flat(o_a), flat(o_b), flat(o_c), flat(o_d),
                    wg_half, wb, wo_half, r1(g_post_mix[l]))
        x2 = _ffn(x2, l, r1(g_pre_ffn[l]), wf_gate, wf_up, wf_down, r1(g_post_ffn[l]))
        x = x2.reshape(B, S, D)
    return x
```

```python
import functools
import math

import jax
import jax.numpy as jnp
import numpy as np
from jax import lax
from jax.experimental import pallas as pl
from jax.experimental.pallas import tpu as pltpu

HEAD_DIM = 64
N_HEADS = 4
PAIR_W = N_HEADS * HEAD_DIM
LANES = 128
PAD_W = N_HEADS * LANES
DIL_PATTERNS = ((128, 1), (512, 4), (2048, 16))
DIL_BLOCK = 128
DIL_UNROLL = 8
RET_CHUNK = 128
MLA_Q_RANK = 256
MLA_KV_RANK = 128
MLA_NOPE = 64
MLA_ROPE = 32
MLA_V = 64
ROPE_THETA = 10000.0
RMS_EPS = 1e-6
GN_EPS = 1e-5
NEG_BIG = -1e30
LOG2E = math.log2(math.e)

MXU_DTYPE = jnp.bfloat16
ROW_TILE = 512
VMEM_LIMIT = 56 * 1024 * 1024

_C_FQ, _C_FK, _C_FV = 0, 256, 512
_C_DQ, _C_DK, _C_DV = 768, 1024, 1280
_C_RQ, _C_RK, _C_RV, _C_RG = 1536, 1792, 2048, 2304
_C_CQ, _C_TAIL = 2560, 2816
IN_CAT_W = 3072
IN_FF_COL = 3 * PAIR_W
IN_KR_COL = 2948


def _f32(x):
    return x.astype(jnp.float32)


def _rms(x, g):
    ms = jnp.mean(x * x, axis=-1, keepdims=True)
    return x * lax.rsqrt(ms + RMS_EPS) * g


def _mm(a, b):
    return jnp.dot(a, b, preferred_element_type=jnp.float32)


def _mm_nt(a, b):
    return lax.dot_general(a, b, (((1,), (1,)), ((), ())), preferred_element_type=jnp.float32)


def _mm_tn(a, b):
    return lax.dot_general(a, b, (((0,), (0,)), ((), ())), preferred_element_type=jnp.float32)


def _lane(shape):
    return lax.broadcasted_iota(jnp.int32, shape, len(shape) - 1)


def _rope_chunks(a, cos, sin_a, sin_b, half):
    outs = []
    for c in range(a.shape[1] // LANES):
        t = a[:, c * LANES:(c + 1) * LANES]
        up = pltpu.roll(t, LANES - half, axis=1)
        dn = pltpu.roll(t, half, axis=1)
        outs.append(t * cos + up * sin_a + dn * sin_b)
    return outs[0] if len(outs) == 1 else jnp.concatenate(outs, axis=1)


def _inproj_kernel(x_ref, g_ref, w_ref, bf_ref, rcos_ref, rsa_ref, rsb_ref, mcos_ref, msa_ref, msb_ref,
                   qn_ref, kvn_ref, wuq_ref, wuk_ref, wuv_ref,
                   fq_ref, fk_ref, fva_ref,
                   dq_ref, dk_ref, dva_ref,
                   rq_ref, rk_ref, rv_ref, sg_ref,
                   mq_ref, mk_ref, mva_ref,
                   carry_ref, wb_ref, *, mla_scale):
    tm = x_ref.shape[0]

    @pl.when((pl.program_id(0) == 0) & (pl.program_id(1) == 0))
    def _():
        n_ff = N_HEADS
        for r in range(0, IN_FF_COL, PAIR_W):
            wb_ref[r:r + PAIR_W, :] = w_ref[r:r + PAIR_W, :].astype(wb_ref.dtype)
        for r in range(IN_FF_COL, _C_TAIL + LANES, LANES):
            wb_ref[r:r + LANES, :] = w_ref[r + n_ff:r + n_ff + LANES, :].astype(wb_ref.dtype)
        blk = _C_TAIL + LANES
        wb_ref[blk:blk + LANES, :] = jnp.zeros((LANES, wb_ref.shape[1]), wb_ref.dtype)
        head = w_ref[IN_FF_COL:IN_FF_COL + 2 * n_ff, :]
        is_ff = lax.broadcasted_iota(jnp.int32, head.shape, 0) < n_ff
        wb_ref[blk:blk + 2 * n_ff, :] = jnp.where(is_ff, head, 0.0).astype(wb_ref.dtype)
        wb_ref[blk + MLA_NOPE:blk + MLA_NOPE + MLA_ROPE, :] = (
            w_ref[IN_KR_COL:IN_KR_COL + MLA_ROPE, :].astype(wb_ref.dtype))

    @pl.when(pl.program_id(1) == 0)
    def _():
        carry_ref[...] = jnp.zeros_like(carry_ref)

    h = _rms(x_ref[...], g_ref[...]).astype(MXU_DTYPE)

    def proj(lo, width):
        return _mm_nt(h, wb_ref[lo:lo + width, :])

    ones_up = jnp.where(_lane((1, PAD_W)) % LANES >= HEAD_DIM, 1.0, 0.0)
    mcos, msa, msb = mcos_ref[...], msa_ref[...], msb_ref[...]
    rcos, rsa, rsb = rcos_ref[...], rsa_ref[...], rsb_ref[...]

    def with_ones(v):
        out = []
        for hd in range(N_HEADS):
            chunk = v[:, (hd // 2) * LANES:(hd // 2 + 1) * LANES]
            if hd % 2:
                chunk = pltpu.roll(chunk, HEAD_DIM, axis=1)
            out.append(jnp.where(_lane(chunk.shape) < HEAD_DIM, chunk, 1.0))
        return jnp.concatenate(out, axis=1)


    tail = proj(_C_TAIL, 2 * LANES)
    cq_raw = proj(_C_CQ, MLA_Q_RANK)
    ffkr = tail[:, LANES:2 * LANES]
    ff = ffkr + bf_ref[...]
    log_f = jnp.minimum(ff, 0.0) - jnp.log1p(jnp.exp(-jnp.abs(ff)))
    rows = lax.broadcasted_iota(jnp.int32, (tm, LANES), 0)
    cs = log_f
    step = 1
    while step < tm:
        cs = cs + jnp.where(rows >= step, pltpu.roll(cs, step, axis=0), 0.0)
        step *= 2
    cs = cs + carry_ref[...]
    carry_ref[...] = cs[tm - 1:tm, :]
    cs2 = cs * LOG2E
    hi = _f32(cs2.astype(MXU_DTYPE))
    rem = cs2 - hi
    mid = _f32(rem.astype(MXU_DTYPE))
    lo = rem - mid
    lane = _lane(cs2.shape)
    split = jnp.where(lane < N_HEADS, hi,
                      jnp.where(lane < 2 * N_HEADS, pltpu.roll(mid, N_HEADS, axis=1),
                                jnp.where(lane < 3 * N_HEADS, pltpu.roll(lo, 2 * N_HEADS, axis=1), 0.0)))
    fk_ref[:, PAIR_W:PAIR_W + LANES] = (-split).astype(fk_ref.dtype)

    rq_ref[...] = _rope_chunks(proj(_C_RQ, PAIR_W), rcos, rsa, rsb, HEAD_DIM // 2).astype(rq_ref.dtype)
    rk_ref[...] = _rope_chunks(proj(_C_RK, PAIR_W), rcos, rsa, rsb, HEAD_DIM // 2) * HEAD_DIM ** -0.5
    rg = proj(_C_RG, PAIR_W)
    sg_ref[...] = rg / (1.0 + jnp.exp(-rg))

    ckv = _rms(tail[:, 0:LANES], kvn_ref[...]).astype(MXU_DTYPE)
    rope_lanes = (_lane(ffkr.shape) >= MLA_NOPE) & (_lane(ffkr.shape) < MLA_NOPE + MLA_ROPE)
    kr = jnp.where(rope_lanes, _rope_chunks(ffkr, mcos, msa, msb, MLA_ROPE // 2), 0.0)
    mk_ref[...] = (_mm(ckv, wuk_ref[...]) + jnp.concatenate([kr] * N_HEADS, axis=1)).astype(mk_ref.dtype)
    mva_ref[...] = (_mm(ckv, wuv_ref[...]) + ones_up).astype(mva_ref.dtype)
    cq = _rms(cq_raw, qn_ref[...]).astype(MXU_DTYPE)
    q = _rope_chunks(_mm(cq, wuq_ref[...]), mcos, msa, msb, MLA_ROPE // 2)
    mq_ref[...] = (q * mla_scale).astype(mq_ref.dtype)

    fva_ref[...] = with_ones(proj(_C_FV, PAIR_W)).astype(fva_ref.dtype)
    dva_ref[...] = with_ones(proj(_C_DV, PAIR_W)).astype(dva_ref.dtype)
    fq_ref[...] = (proj(_C_FQ, PAIR_W) * (HEAD_DIM ** -0.5 * LOG2E)).astype(fq_ref.dtype)
    dq_ref[...] = (proj(_C_DQ, PAIR_W) * (HEAD_DIM ** -0.5 * LOG2E)).astype(dq_ref.dtype)

    fk_ref[:, 0:PAIR_W] = proj(_C_FK, PAIR_W).astype(fk_ref.dtype)
    dk_ref[...] = proj(_C_DK, PAIR_W).astype(dk_ref.dtype)
    rv_ref[...] = proj(_C_RV, PAIR_W).astype(rv_ref.dtype)


def _layer(a, l, n_grid):
    zeros = (0,) * (a.ndim - 1)
    index = (lambda b, i: (l,) + zeros) if n_grid == 2 else (lambda i: (l,) + zeros)
    return pl.BlockSpec((None,) + a.shape[1:], index, pipeline_mode=pl.Buffered(1))


def _inproj(x, g, w_in, bf, ret_tabs, mla_tabs, qn, kvn, wuq, wuk, wuv):
    B, S, D = x.shape
    tm = ROW_TILE
    ns = S // tm
    row = lambda w: pl.BlockSpec((None, tm, w), lambda b, i: (b, i, 0))
    full = lambda a: pl.BlockSpec(a.shape, lambda b, i: (0,) * a.ndim, pipeline_mode=pl.Buffered(1))
    tab = pl.BlockSpec((tm, LANES), lambda b, i: (i, 0))
    bf16, f32 = MXU_DTYPE, jnp.float32
    out_defs = [
        (PAIR_W, bf16), (PAIR_W + LANES, bf16), (PAD_W, bf16),
        (PAIR_W, bf16), (PAIR_W, bf16), (PAD_W, bf16),
        (PAIR_W, bf16), (PAIR_W, f32), (PAIR_W, bf16), (PAIR_W, f32),
        (PAD_W, bf16), (PAD_W, bf16), (PAD_W, bf16),
    ]
    out_shape, out_specs = [], []
    for d in out_defs:
        out_shape.append(jax.ShapeDtypeStruct((B, S, d[0]), d[1]))
        out_specs.append(row(d[0]))
    tabs = list(ret_tabs) + list(mla_tabs)
    tail = [qn, kvn, wuq, wuk, wuv]
    return pl.pallas_call(
        functools.partial(_inproj_kernel, mla_scale=(MLA_NOPE + MLA_ROPE) ** -0.5 * LOG2E),
        grid=(B, ns),
        in_specs=[row(D), full(g), full(w_in), full(bf)] + [tab] * 6 + [full(a) for a in tail],
        out_specs=out_specs,
        out_shape=out_shape,
        scratch_shapes=[pltpu.VMEM((1, LANES), f32), pltpu.VMEM((IN_CAT_W, D), bf16)],
        compiler_params=pltpu.CompilerParams(
            dimension_semantics=("arbitrary", "arbitrary"), vmem_limit_bytes=VMEM_LIMIT),
        name="inproj",
    )(x, g, w_in, bf, *tabs, *tail)


def _head_q(q_ref, h, pair_q):
    if pair_q:
        g, e = divmod(h, 2)
        q2 = q_ref[:, g * LANES:(g + 1) * LANES]
        keep = (_lane(q2.shape) >= HEAD_DIM) if e else (_lane(q2.shape) < HEAD_DIM)
        return jnp.where(keep, q2, jnp.zeros_like(q2)), g
    return q_ref[:, h * LANES:(h + 1) * LANES], h


def _pack_pair(o_even, o_odd):
    return jnp.where(_lane(o_even.shape) < HEAD_DIM, o_even, pltpu.roll(o_odd, HEAD_DIM, axis=1))


def _flash_kernel(q_ref, k_ref, va_ref, o_ref, qop_ref, m_ref, acc_ref, s0_ref, *, fox):
    tq = q_ref.shape[0]
    tk = tq
    qi = pl.program_id(1)
    for h in range(N_HEADS):
        if fox:
            qm, _ = _head_q(q_ref, h, True)
            lane = _lane(qm.shape)
            ones = jnp.where((lane % N_HEADS == h) & (lane < 3 * N_HEADS), 1.0, 0.0).astype(qm.dtype)
            qop_ref[h] = jnp.concatenate([qm, ones], axis=1)
        else:
            qop_ref[h] = q_ref[:, h * LANES:(h + 1) * LANES]
    m_ref[...] = jnp.full(m_ref.shape, NEG_BIG, jnp.float32)
    acc_ref[...] = jnp.zeros(acc_ref.shape, jnp.float32)

    def keys(start, h):
        if fox:
            g = h // 2
            return jnp.concatenate([k_ref[pl.ds(start, tk), g * LANES:(g + 1) * LANES],
                                    k_ref[pl.ds(start, tk), PAIR_W:PAIR_W + LANES]], axis=1)
        return k_ref[pl.ds(start, tk), h * LANES:(h + 1) * LANES]

    def step(j, diagonal):
        start = pl.multiple_of(j * tk, tk)
        s_next = s0_ref[...]
        for h in range(N_HEADS):
            s = s_next
            if h + 1 < N_HEADS:
                s_next = _mm_nt(qop_ref[h + 1], keys(start, h + 1))
            elif not diagonal:
                s0_ref[...] = _mm_nt(qop_ref[0], keys(pl.multiple_of((j + 1) * tk, tk), 0))
            if diagonal:
                r = lax.broadcasted_iota(jnp.int32, s.shape, 0)
                c = lax.broadcasted_iota(jnp.int32, s.shape, 1)
                s = jnp.where(c <= r, s, NEG_BIG)
            m = m_ref[h]
            m_new = jnp.maximum(m, jnp.max(s, axis=-1, keepdims=True))
            p = jnp.concatenate([jnp.exp2(s[:, c * LANES:(c + 1) * LANES] - m_new) for c in range(tk // LANES)],
                                axis=1).astype(MXU_DTYPE)
            acc_ref[h] = (jnp.exp2(m - m_new) * acc_ref[h]
                          + _mm(p, va_ref[pl.ds(start, tk), h * LANES:(h + 1) * LANES]))
            m_ref[h] = m_new

    def body(j, carry):
        step(j, False)
        return carry

    s0_ref[...] = _mm_nt(qop_ref[0], keys(0, 0))
    lax.fori_loop(0, qi, body, 0)
    step(qi, True)
    outs = []
    for h in range(N_HEADS):
        acc = acc_ref[h]
        outs.append(acc / pltpu.roll(acc, HEAD_DIM, axis=1))
    for g in range(N_HEADS // 2):
        o_ref[:, g * LANES:(g + 1) * LANES] = _pack_pair(outs[2 * g], outs[2 * g + 1]).astype(o_ref.dtype)


def _flash(q, k, va, fox):
    B, S, qw = q.shape
    tq = ROW_TILE
    seq = lambda w: pl.BlockSpec((None, S, w), lambda b, i: (b, 0, 0))
    return pl.pallas_call(
        functools.partial(_flash_kernel, fox=fox),
        grid=(B, S // tq),
        in_specs=[pl.BlockSpec((None, tq, qw), lambda b, i: (b, i, 0)), seq(k.shape[-1]), seq(PAD_W)],
        out_specs=pl.BlockSpec((None, tq, PAIR_W), lambda b, i: (b, i, 0)),
        out_shape=jax.ShapeDtypeStruct((B, S, PAIR_W), MXU_DTYPE),
        scratch_shapes=[pltpu.VMEM((N_HEADS, tq, 2 * LANES if fox else LANES), MXU_DTYPE),
                        pltpu.VMEM((N_HEADS, tq, LANES), jnp.float32),
                        pltpu.VMEM((N_HEADS, tq, LANES), jnp.float32),
                        pltpu.VMEM((tq, tq), jnp.float32)],
        compiler_params=pltpu.CompilerParams(
            dimension_semantics=("arbitrary", "arbitrary"), vmem_limit_bytes=VMEM_LIMIT),
        name="fox_attn" if fox else "mla_attn",
    )(q, k, va)


def _dilated_kernel(q_ref, k_ref, va_ref, o_ref, qf_ref, kf_ref, vf_ref, acc_ref, stage_ref):
    S = q_ref.shape[0]
    n = DIL_BLOCK
    tile = ROW_TILE
    quarter = S // 4
    run = n // 4
    keep_stats = _lane((n, LANES)) < HEAD_DIM + HEAD_DIM // 2
    stat0 = jnp.where(_lane((tile, LANES)) < HEAD_DIM + HEAD_DIM // 2, 0.0, NEG_BIG)

    def widen(i, carry):
        rows = pl.ds(pl.multiple_of(i * tile, tile), tile)
        slot = 0
        for src, dst in ((q_ref, qf_ref), (k_ref, kf_ref), (va_ref, vf_ref)):
            for c in range(dst.shape[0]):
                stage_ref[slot] = _f32(src[rows, c * LANES:(c + 1) * LANES])
                for cls in range(4):
                    dst[c, pl.ds(cls * quarter + i * (tile // 4), tile // 4), :] = (
                        stage_ref[slot, pl.ds(cls, tile // 4, stride=4), :])
                slot += 1
        for h in range(N_HEADS):
            acc_ref[h, rows, :] = stat0
        return carry

    lax.fori_loop(0, S // tile, widen, 0)

    ri = lax.broadcasted_iota(jnp.int32, (n, 2 * n), 0)
    ci = lax.broadcasted_iota(jnp.int32, (n, 2 * n), 1)
    diff_class = ri - ci
    diff_runs = (4 * (jnp.bitwise_and(ri, run - 1) - jnp.bitwise_and(ci, 2 * run - 1))
                 + jnp.right_shift(ri, run.bit_length() - 1) - jnp.right_shift(ci, (2 * run).bit_length() - 1))

    def plan(dil, u):
        if dil == 1:
            first = u == 0
            q_runs = [(cls * quarter + run * u, run, 1) for cls in range(4)]
            w_runs = [(cls * quarter + run * jnp.maximum(u - 1, 0), 2 * run, 1) for cls in range(4)]
            return q_runs, w_runs, diff_runs, first
        if dil == 4:
            cls, a = jnp.bitwise_and(u, 3), jnp.right_shift(u, 2)
            base, step, stride = cls * quarter, n, 1
        else:
            rho, a = jnp.bitwise_and(u, 15), jnp.right_shift(u, 4)
            base, step, stride = jnp.bitwise_and(rho, 3) * quarter + jnp.right_shift(rho, 2), 4 * n, 4
        q_runs = [(base + step * a, n, stride)]
        w_runs = [(base + step * jnp.maximum(a - 1, 0), 2 * n, stride)]
        return q_runs, w_runs, diff_class, a == 0

    def rows_of(r):
        start, size, stride = r
        return pl.ds(start, size, stride=stride) if stride > 1 else pl.ds(start, size)

    def load(ref, idx, runs):
        parts = [ref[idx, rows_of(r), :] for r in runs]
        return parts[0] if len(parts) == 1 else jnp.concatenate(parts, axis=0)

    def store(ref, idx, runs, val):
        at = 0
        for r in runs:
            ref[idx, rows_of(r), :] = val[at:at + r[1], :]
            at += r[1]

    for _, dil in DIL_PATTERNS:

        def blocks(t, carry, dil=dil):
            todo = []
            for sub in range(DIL_UNROLL):
                q_runs, w_runs, diff, first = plan(dil, t * DIL_UNROLL + sub)
                dist = diff + jnp.where(first, 0, n)
                valid = (dist >= 0) & (dist <= n)
                q2 = [load(qf_ref, g, q_runs).astype(MXU_DTYPE) for g in range(N_HEADS // 2)]
                k2 = [load(kf_ref, g, w_runs).astype(MXU_DTYPE) for g in range(N_HEADS // 2)]
                v4 = [load(vf_ref, h, w_runs).astype(MXU_DTYPE) for h in range(N_HEADS)]
                old = [load(acc_ref, h, q_runs) for h in range(N_HEADS)]
                m_old = [jnp.broadcast_to(o[:, LANES - 1:LANES], (n, LANES)) for o in old]
                todo.append((q_runs, valid, q2, k2, v4, old, m_old))
            done = []
            for q_runs, valid, q2, k2, v4, old, m_old in todo:
                for h in range(N_HEADS):
                    g, e = divmod(h, 2)
                    keep = (_lane(q2[g].shape) >= HEAD_DIM) if e else (_lane(q2[g].shape) < HEAD_DIM)
                    qh = jnp.where(keep, q2[g], jnp.zeros_like(q2[g]))
                    s = jnp.where(valid, _mm_nt(qh, k2[g]), NEG_BIG)
                    m_new = jnp.maximum(m_old[h], jnp.max(s, axis=-1, keepdims=True))
                    p = jnp.concatenate([jnp.exp2(s[:, c * LANES:(c + 1) * LANES] - m_new) for c in range(2)],
                                        axis=1).astype(MXU_DTYPE)
                    upd = jnp.exp2(m_old[h] - m_new) * old[h] + _mm(p, v4[h])
                    done.append((h, q_runs, jnp.where(keep_stats, upd, m_new)))
            for h, q_runs, val in done:
                store(acc_ref, h, q_runs, val)
            return carry

        lax.fori_loop(0, S // n // DIL_UNROLL, blocks, 0)

    def finish(i, carry):
        rows = pl.ds(pl.multiple_of(i * tile, tile), tile)
        outs = []
        for h in range(N_HEADS):
            for cls in range(4):
                a_h = acc_ref[h, pl.ds(cls * quarter + i * (tile // 4), tile // 4), :]
                stage_ref[h, pl.ds(cls, tile // 4, stride=4), :] = a_h / a_h[:, HEAD_DIM:HEAD_DIM + 1]
            outs.append(stage_ref[h])
        for g in range(N_HEADS // 2):
            o_ref[rows, g * LANES:(g + 1) * LANES] = _pack_pair(outs[2 * g], outs[2 * g + 1]).astype(o_ref.dtype)
        return carry

    lax.fori_loop(0, S // tile, finish, 0)


def _dilated(q, k, va):
    B, S, _ = q.shape
    assert S % (DIL_PATTERNS[-1][1] * 2 * DIL_BLOCK) == 0 and S % ROW_TILE == 0
    seq = lambda w: pl.BlockSpec((None, S, w), lambda b: (b, 0, 0))
    f32 = jnp.float32
    n_chunks = (2 * PAIR_W + PAD_W) // LANES
    return pl.pallas_call(
        _dilated_kernel,
        grid=(B,),
        in_specs=[seq(PAIR_W), seq(PAIR_W), seq(PAD_W)],
        out_specs=seq(PAIR_W),
        out_shape=jax.ShapeDtypeStruct((B, S, PAIR_W), MXU_DTYPE),
        scratch_shapes=[pltpu.VMEM((PAIR_W // LANES, S, LANES), f32), pltpu.VMEM((PAIR_W // LANES, S, LANES), f32),
                        pltpu.VMEM((N_HEADS, S, LANES), f32), pltpu.VMEM((N_HEADS, S, LANES), f32),
                        pltpu.VMEM((n_chunks, ROW_TILE, LANES), f32)],
        compiler_params=pltpu.CompilerParams(dimension_semantics=("arbitrary",), vmem_limit_bytes=VMEM_LIMIT),
        name="dilated",
    )(q, k, va)


_RET_LOG_G = [math.log1p(-(2.0 ** (-5.0 - h))) for h in range(N_HEADS)]


def _retention_tables(T, decay_ref, xi_ref, zeta_ref):
    lane = _lane((T, LANES))
    low = lane < HEAD_DIM
    pos = _f32(lax.broadcasted_iota(jnp.int32, (T, LANES), 0))
    for c in range(T // LANES):
        rel = pos - _f32(lane) - float(c * LANES)
        for h in range(N_HEADS):
            decay_ref[h, :, c * LANES:(c + 1) * LANES] = jnp.where(
                rel >= 0, jnp.exp(jnp.maximum(rel, 0.0) * _RET_LOG_G[h]), 0.0)
    for g in range(N_HEADS // 2):
        lg_lane = jnp.where(low, _RET_LOG_G[2 * g], _RET_LOG_G[2 * g + 1])
        xi_ref[g] = jnp.exp((pos + 1.0) * lg_lane)
        zeta_ref[g] = jnp.exp((T - 1.0 - pos) * lg_lane)


def _retention_scores(g, q2, kf, v2, state_ref, decay_ref, xi_ref, zeta_ref):
    T = q2.shape[0]
    low = _lane((T, LANES)) < HEAD_DIM
    row_low = lax.broadcasted_iota(jnp.int32, (LANES, LANES), 0) < HEAD_DIM
    same_head = row_low == (_lane((LANES, LANES)) < HEAD_DIM)
    k2 = kf.astype(MXU_DTYPE)
    R = state_ref[g]
    cross = _mm(q2, R.astype(MXU_DTYPE)) * xi_ref[g]
    kv = _mm_tn((kf * zeta_ref[g]).astype(MXU_DTYPE), v2)
    g_rows = jnp.where(row_low, math.exp(T * _RET_LOG_G[2 * g]), math.exp(T * _RET_LOG_G[2 * g + 1]))
    state_ref[g] = g_rows * R + jnp.where(same_head, kv, 0.0)
    scores = []
    for e in range(2):
        qm = jnp.where(low if e == 0 else ~low, q2, jnp.zeros_like(q2))
        scores.append((_mm_nt(qm, k2) * decay_ref[2 * g + e]).astype(MXU_DTYPE))
    return scores, cross


def _retention_output(scores, cross, v2, sg, gain):
    low = _lane(cross.shape) < HEAD_DIM
    out = jnp.where(low, _mm(scores[0], v2), _mm(scores[1], v2)) + cross
    mu = jnp.where(low,
                   jnp.sum(jnp.where(low, out, 0.0), axis=-1, keepdims=True),
                   jnp.sum(jnp.where(low, 0.0, out), axis=-1, keepdims=True)) * (1.0 / HEAD_DIM)
    dlt = out - mu
    sq = dlt * dlt
    var = jnp.where(low,
                    jnp.sum(jnp.where(low, sq, 0.0), axis=-1, keepdims=True),
                    jnp.sum(jnp.where(low, 0.0, sq), axis=-1, keepdims=True)) * (1.0 / HEAD_DIM)
    return (sg * (dlt * lax.rsqrt(var + GN_EPS) * gain)).astype(MXU_DTYPE)


def _merge_kernel(x_ref, gpre_ref, oa_ref, ob_ref, od_ref, rq_ref, rk_ref, rv_ref, sg_ref, gain_ref,
                  wg_ref, wb_ref, wo_ref, gpost_ref, y_ref, state_ref, decay_ref, xi_ref, zeta_ref,
                  *, tiles_per_seq):
    T, D = x_ref.shape

    @pl.when(pl.program_id(0) == 0)
    def _():
        _retention_tables(T, decay_ref, xi_ref, zeta_ref)

    @pl.when(pl.program_id(0) % tiles_per_seq == 0)
    def _():
        state_ref[...] = jnp.zeros_like(state_ref)

    half = T // 2
    rows = [slice(0, half), slice(half, 2 * half)]
    xs = [x_ref[r, :] for r in rows]
    hs = [_rms(x, gpre_ref[...]).astype(MXU_DTYPE) for x in xs]
    twice_merged = [None, None]

    def add_branch(n, branch):
        for k, (r, h) in enumerate(zip(rows, hs)):
            t = jnp.tanh(_mm(h, wg_ref[:, n * D:(n + 1) * D]))
            proj = _mm(branch[r, :], wb_ref[n])
            term = proj * t + proj
            twice_merged[k] = term if twice_merged[k] is None else twice_merged[k] + term

    pair_cols = [slice(g * LANES, (g + 1) * LANES) for g in range(N_HEADS // 2)]
    vals = [rv_ref[:, c] for c in pair_cols]
    ret = [_retention_scores(g, rq_ref[:, c], rk_ref[:, c], vals[g], state_ref, decay_ref, xi_ref, zeta_ref)
           for g, c in enumerate(pair_cols)]
    add_branch(0, oa_ref)
    add_branch(1, ob_ref)
    o_c = jnp.concatenate(
        [_retention_output(ret[g][0], ret[g][1], vals[g], sg_ref[:, c], gain_ref[:, c])
         for g, c in enumerate(pair_cols)], axis=1)
    add_branch(3, od_ref)
    add_branch(2, o_c)
    for r, x, tm2 in zip(rows, xs, twice_merged):
        mix = _mm(tm2.astype(MXU_DTYPE), wo_ref[...])
        y_ref[r, :] = x + _rms(mix, gpost_ref[...])


def _merge(x2, l, tiles_per_seq, gpre, oa, ob, od, rq, rk, rv, sg, gain, wg_half, wb, wo_half, gpost):
    N, D = x2.shape
    tm = ROW_TILE
    f32 = jnp.float32
    row = lambda w: pl.BlockSpec((tm, w), lambda i: (i, 0))
    full = lambda a: pl.BlockSpec(a.shape, lambda i: (0,) * a.ndim, pipeline_mode=pl.Buffered(1))
    wg, wo = wg_half, wo_half
    return pl.pallas_call(
        functools.partial(_merge_kernel, tiles_per_seq=tiles_per_seq),
        grid=(N // tm,),
        in_specs=[row(D), full(gpre), row(PAIR_W), row(PAIR_W), row(PAIR_W),
                  row(PAIR_W), row(PAIR_W), row(PAIR_W), row(PAIR_W), full(gain),
                  _layer(wg, l, 1), _layer(wb, l, 1), _layer(wo, l, 1), full(gpost)],
        out_specs=row(D),
        out_shape=jax.ShapeDtypeStruct((N, D), jnp.float32),
        scratch_shapes=[pltpu.VMEM((N_HEADS // 2, LANES, LANES), f32), pltpu.VMEM((N_HEADS, tm, tm), f32),
                        pltpu.VMEM((N_HEADS // 2, tm, LANES), f32), pltpu.VMEM((N_HEADS // 2, tm, LANES), f32)],
        compiler_params=pltpu.CompilerParams(dimension_semantics=("arbitrary",), vmem_limit_bytes=VMEM_LIMIT),
        name="merge",
    )(x2, gpre, oa, ob, od, rq, rk, rv, sg, gain, wg, wb, wo, gpost)


def _ffn_kernel(x_ref, gpre_ref, wgate_ref, wup_ref, wdown_ref, gpost_ref, y_ref):
    half = x_ref.shape[0] // 2
    rows = [slice(0, half), slice(half, 2 * half)]
    xs = [x_ref[r, :] for r in rows]
    hs = [_rms(x, gpre_ref[...]).astype(MXU_DTYPE) for x in xs]
    acts = []
    for h in hs:
        gate = _mm(h, wgate_ref[...])
        acts.append((gate / (1.0 + jnp.exp(-gate)) * _mm(h, wup_ref[...])).astype(MXU_DTYPE))
    for r, x, act in zip(rows, xs, acts):
        y_ref[r, :] = x + _rms(_mm(act, wdown_ref[...]), gpost_ref[...])


def _ffn(x2, l, gpre, wgate, wup, wdown, gpost):
    N, D = x2.shape
    tm = ROW_TILE
    row = pl.BlockSpec((tm, D), lambda i: (i, 0))
    full = lambda a: pl.BlockSpec(a.shape, lambda i: (0,) * a.ndim, pipeline_mode=pl.Buffered(1))
    return pl.pallas_call(
        _ffn_kernel,
        grid=(N // tm,),
        in_specs=[row, full(gpre), _layer(wgate, l, 1), _layer(wup, l, 1), _layer(wdown, l, 1), full(gpost)],
        out_specs=row,
        out_shape=jax.ShapeDtypeStruct((N, D), jnp.float32),
        compiler_params=pltpu.CompilerParams(dimension_semantics=("arbitrary",), vmem_limit_bytes=VMEM_LIMIT),
        name="ffn",
    )(x2, gpre, wgate, wup, wdown, gpost)


def _rope_tables(S, half, lo):
    period = 2 * half if lo == 0 else LANES
    inv = ROPE_THETA ** (-jnp.arange(half, dtype=jnp.float32) / half)
    ang = jnp.arange(S, dtype=jnp.float32)[:, None] * inv[None, :]
    cos, sin = jnp.cos(ang), jnp.sin(ang)
    reps = LANES // period
    pad_lo = jnp.zeros((S, lo), jnp.float32)
    pad_hi = jnp.zeros((S, period - lo - 2 * half), jnp.float32)
    zeros = jnp.zeros((S, half), jnp.float32)
    cos_t = jnp.concatenate([pad_lo + 1.0, cos, cos, pad_hi + 1.0], axis=1)
    sin_up = jnp.concatenate([pad_lo, -sin, zeros, pad_hi], axis=1)
    sin_dn = jnp.concatenate([pad_lo, zeros, sin, pad_hi], axis=1)
    return tuple(jnp.tile(t, (1, reps)) for t in (cos_t, sin_up, sin_dn))


def _pad_heads(w, width=HEAD_DIM):
    K = w.shape[0]
    w = w.reshape(K, N_HEADS, width)
    return jnp.pad(w, ((0, 0), (0, 0), (0, LANES - width))).reshape(K, PAD_W)


def _layer_weights(b_forget, w_uq, w_ukv):
    bf = jnp.pad(b_forget, (0, LANES - N_HEADS)).reshape(1, LANES)
    wuq = _pad_heads(w_uq, MLA_NOPE + MLA_ROPE).astype(MXU_DTYPE)
    ukv = w_ukv.reshape(MLA_KV_RANK, N_HEADS, MLA_NOPE + MLA_V)
    wuk = _pad_heads(ukv[:, :, :MLA_NOPE].reshape(MLA_KV_RANK, -1)).astype(MXU_DTYPE)
    wuv = _pad_heads(ukv[:, :, MLA_NOPE:].reshape(MLA_KV_RANK, -1)).astype(MXU_DTYPE)
    return bf, wuq, wuk, wuv


def kernel(x, w_in, b_forget, ret_gn_gain, mla_q_norm, mla_kv_norm, w_uq, w_ukv, w_gate, w_branch, w_out,
           g_pre_mix, g_post_mix, g_pre_ffn, g_post_ffn, w_ffn_gate, w_ffn_up, w_ffn_down):
    B, S, D = x.shape
    depth = w_in.shape[0]
    assert S % ROW_TILE == 0
    ret_tabs = _rope_tables(S, HEAD_DIM // 2, 0)
    mla_tabs = _rope_tables(S, MLA_ROPE // 2, MLA_NOPE)
    r1 = lambda v: v.reshape(1, -1)
    assert w_in.shape[2] == IN_KR_COL + MLA_ROPE
    wg_half = (0.5 * w_gate).astype(MXU_DTYPE)
    wo_half = (0.5 * w_out).astype(MXU_DTYPE)
    wb = w_branch.astype(MXU_DTYPE)
    wf_gate, wf_up, wf_down = (w.astype(MXU_DTYPE) for w in (w_ffn_gate, w_ffn_up, w_ffn_down))
    for l in range(depth):
        bf, wuq, wuk, wuv = _layer_weights(b_forget[l], w_uq[l], w_ukv[l])
        (fq, fk, fva, dq, dk, dva, rq, rk, rv, sg, mq, mk, mva) = _inproj(
            x, r1(g_pre_mix[l]), w_in[l].T, bf, ret_tabs, mla_tabs,
            r1(mla_q_norm[l]), r1(mla_kv_norm[l]), wuq, wuk, wuv)
        o_a = _flash(fq, fk, fva, True)
        o_b = _dilated(dq, dk, dva)
        o_d = _flash(mq, mk, mva, False)
        flat = lambda a: a.reshape(B * S, a.shape[-1])
        x2 = _merge(flat(x), l, S // ROW_TILE, r1(g_pre_mix[l]), flat(o_a), flat(o_b), flat(o_d),
                    flat(rq), flat(rk), flat(rv), flat(sg), r1(ret_gn_gain[l]),
                    wg_half, wb, wo_half, r1(g_post_mix[l]))
        x2 = _ffn(x2, l, r1(g_pre_ffn[l]), wf_gate, wf_up, wf_down, r1(g_post_ffn[l]))
        x = x2.reshape(B, S, D)
    return x
```

```python
import functools
import math

import jax
import jax.numpy as jnp
import numpy as np
from jax import lax
from jax.experimental import pallas as pl
from jax.experimental.pallas import tpu as pltpu

HEAD_DIM = 64
N_HEADS = 4
PAIR_W = N_HEADS * HEAD_DIM
LANES = 128
PAD_W = N_HEADS * LANES
DIL_PATTERNS = ((128, 1), (512, 4), (2048, 16))
DIL_BLOCK = 128
DIL_UNROLL = 8
RET_CHUNK = 128
MLA_Q_RANK = 256
MLA_KV_RANK = 128
MLA_NOPE = 64
MLA_ROPE = 32
MLA_V = 64
ROPE_THETA = 10000.0
RMS_EPS = 1e-6
GN_EPS = 1e-5
NEG_BIG = -1e30
LOG2E = math.log2(math.e)

MXU_DTYPE = jnp.bfloat16
ROW_TILE = 512
VMEM_LIMIT = 56 * 1024 * 1024

_C_FQ, _C_FK, _C_FV = 0, 256, 512
_C_DQ, _C_DK, _C_DV = 768, 1024, 1280
_C_RQ, _C_RK, _C_RV, _C_RG = 1536, 1792, 2048, 2304
_C_CQ, _C_TAIL = 2560, 2816
IN_CAT_W = 3072
IN_FF_COL = 3 * PAIR_W
IN_KR_COL = 2948


def _f32(x):
    return x.astype(jnp.float32)


def _rms(x, g):
    ms = jnp.mean(x * x, axis=-1, keepdims=True)
    return x * lax.rsqrt(ms + RMS_EPS) * g


def _mm(a, b):
    return jnp.dot(a, b, preferred_element_type=jnp.float32)


def _mm_nt(a, b):
    return lax.dot_general(a, b, (((1,), (1,)), ((), ())), preferred_element_type=jnp.float32)


def _mm_tn(a, b):
    return lax.dot_general(a, b, (((0,), (0,)), ((), ())), preferred_element_type=jnp.float32)


def _lane(shape):
    return lax.broadcasted_iota(jnp.int32, shape, len(shape) - 1)


def _rope_chunks(a, cos, sin_a, sin_b, half):
    outs = []
    for c in range(a.shape[1] // LANES):
        t = a[:, c * LANES:(c + 1) * LANES]
        up = pltpu.roll(t, LANES - half, axis=1)
        dn = pltpu.roll(t, half, axis=1)
        outs.append(t * cos + up * sin_a + dn * sin_b)
    return outs[0] if len(outs) == 1 else jnp.concatenate(outs, axis=1)


def _inproj_kernel(x_ref, g_ref, w_ref, bf_ref, rcos_ref, rsa_ref, rsb_ref, mcos_ref, msa_ref, msb_ref,
                   qn_ref, kvn_ref, wuq_ref, wuk_ref, wuv_ref,
                   fq_ref, fk_ref, fva_ref,
                   dq_ref, dk_ref, dva_ref,
                   rq_ref, rk_ref, rv_ref, sg_ref,
                   mq_ref, mk_ref, mva_ref,
                   carry_ref, wb_ref, *, mla_scale):
    tm = x_ref.shape[0]

    @pl.when((pl.program_id(0) == 0) & (pl.program_id(1) == 0))
    def _():
        n_ff = N_HEADS
        for r in range(0, IN_FF_COL, PAIR_W):
            wb_ref[r:r + PAIR_W, :] = w_ref[r:r + PAIR_W, :].astype(wb_ref.dtype)
        for r in range(IN_FF_COL, _C_TAIL + LANES, LANES):
            wb_ref[r:r + LANES, :] = w_ref[r + n_ff:r + n_ff + LANES, :].astype(wb_ref.dtype)
        blk = _C_TAIL + LANES
        wb_ref[blk:blk + LANES, :] = jnp.zeros((LANES, wb_ref.shape[1]), wb_ref.dtype)
        head = w_ref[IN_FF_COL:IN_FF_COL + 2 * n_ff, :]
        is_ff = lax.broadcasted_iota(jnp.int32, head.shape, 0) < n_ff
        wb_ref[blk:blk + 2 * n_ff, :] = jnp.where(is_ff, head, 0.0).astype(wb_ref.dtype)
        wb_ref[blk + MLA_NOPE:blk + MLA_NOPE + MLA_ROPE, :] = (
            w_ref[IN_KR_COL:IN_KR_COL + MLA_ROPE, :].astype(wb_ref.dtype))

    @pl.when(pl.program_id(1) == 0)
    def _():
        carry_ref[...] = jnp.zeros_like(carry_ref)

    h = _rms(x_ref[...], g_ref[...]).astype(MXU_DTYPE)

    def proj(lo, width):
        return _mm_nt(h, wb_ref[lo:lo + width, :])

    ones_up = jnp.where(_lane((1, PAD_W)) % LANES >= HEAD_DIM, 1.0, 0.0)
    mcos, msa, msb = mcos_ref[...], msa_ref[...], msb_ref[...]
    rcos, rsa, rsb = rcos_ref[...], rsa_ref[...], rsb_ref[...]

    def with_ones(v):
        out = []
        for hd in range(N_HEADS):
            chunk = v[:, (hd // 2) * LANES:(hd // 2 + 1) * LANES]
            if hd % 2:
                chunk = pltpu.roll(chunk, HEAD_DIM, axis=1)
            out.append(jnp.where(_lane(chunk.shape) < HEAD_DIM, chunk, 1.0))
        return jnp.concatenate(out, axis=1)


    tail = proj(_C_TAIL, 2 * LANES)
    cq_raw = proj(_C_CQ, MLA_Q_RANK)
    ffkr = tail[:, LANES:2 * LANES]
    ff = ffkr + bf_ref[...]
    log_f = jnp.minimum(ff, 0.0) - jnp.log1p(jnp.exp(-jnp.abs(ff)))
    rows = lax.broadcasted_iota(jnp.int32, (tm, LANES), 0)
    cs = log_f
    step = 1
    while step < tm:
        cs = cs + jnp.where(rows >= step, pltpu.roll(cs, step, axis=0), 0.0)
        step *= 2
    cs = cs + carry_ref[...]
    carry_ref[...] = cs[tm - 1:tm, :]
    cs2 = cs * LOG2E
    hi = _f32(cs2.astype(MXU_DTYPE))
    rem = cs2 - hi
    mid = _f32(rem.astype(MXU_DTYPE))
    lo = rem - mid
    lane = _lane(cs2.shape)
    split = jnp.where(lane < N_HEADS, hi,
                      jnp.where(lane < 2 * N_HEADS, pltpu.roll(mid, N_HEADS, axis=1),
                                jnp.where(lane < 3 * N_HEADS, pltpu.roll(lo, 2 * N_HEADS, axis=1), 0.0)))
    fk_ref[:, PAIR_W:PAIR_W + LANES] = (-split).astype(fk_ref.dtype)

    rq_ref[...] = _rope_chunks(proj(_C_RQ, PAIR_W), rcos, rsa, rsb, HEAD_DIM // 2).astype(rq_ref.dtype)
    rk_ref[...] = _rope_chunks(proj(_C_RK, PAIR_W), rcos, rsa, rsb, HEAD_DIM // 2) * HEAD_DIM ** -0.5
    rg = proj(_C_RG, PAIR_W)
    sg_ref[...] = rg / (1.0 + jnp.exp(-rg))

    ckv = _rms(tail[:, 0:LANES], kvn_ref[...]).astype(MXU_DTYPE)
    rope_lanes = (_lane(ffkr.shape) >= MLA_NOPE) & (_lane(ffkr.shape) < MLA_NOPE + MLA_ROPE)
    kr = jnp.where(rope_lanes, _rope_chunks(ffkr, mcos, msa, msb, MLA_ROPE // 2), 0.0)
    mk_ref[...] = (_mm(ckv, wuk_ref[...]) + jnp.concatenate([kr] * N_HEADS, axis=1)).astype(mk_ref.dtype)
    mva_ref[...] = (_mm(ckv, wuv_ref[...]) + ones_up).astype(mva_ref.dtype)
    cq = _rms(cq_raw, qn_ref[...]).astype(MXU_DTYPE)
    q = _rope_chunks(_mm(cq, wuq_ref[...]), mcos, msa, msb, MLA_ROPE // 2)
    mq_ref[...] = (q * mla_scale).astype(mq_ref.dtype)

    fva_ref[...] = with_ones(proj(_C_FV, PAIR_W)).astype(fva_ref.dtype)
    dva_ref[...] = with_ones(proj(_C_DV, PAIR_W)).astype(dva_ref.dtype)
    fq_ref[...] = (proj(_C_FQ, PAIR_W) * (HEAD_DIM ** -0.5 * LOG2E)).astype(fq_ref.dtype)
    dq_ref[...] = (proj(_C_DQ, PAIR_W) * (HEAD_DIM ** -0.5 * LOG2E)).astype(dq_ref.dtype)

    fk_ref[:, 0:PAIR_W] = proj(_C_FK, PAIR_W).astype(fk_ref.dtype)
    dk_ref[...] = proj(_C_DK, PAIR_W).astype(dk_ref.dtype)
    rv_ref[...] = proj(_C_RV, PAIR_W).astype(rv_ref.dtype)


def _layer(a, l, n_grid):
    zeros = (0,) * (a.ndim - 1)
    index = (lambda b, i: (l,) + zeros) if n_grid == 2 else (lambda i: (l,) + zeros)
    return pl.BlockSpec((None,) + a.shape[1:], index, pipeline_mode=pl.Buffered(1))


def _inproj(x, g, w_in, bf, ret_tabs, mla_tabs, qn, kvn, wuq, wuk, wuv):
    B, S, D = x.shape
    tm = ROW_TILE
    ns = S // tm
    row = lambda w: pl.BlockSpec((None, tm, w), lambda b, i: (b, i, 0))
    full = lambda a: pl.BlockSpec(a.shape, lambda b, i: (0,) * a.ndim, pipeline_mode=pl.Buffered(1))
    tab = pl.BlockSpec((tm, LANES), lambda b, i: (i, 0))
    bf16, f32 = MXU_DTYPE, jnp.float32
    out_defs = [
        (PAIR_W, bf16), (PAIR_W + LANES, bf16), (PAD_W, bf16),
        (PAIR_W, bf16), (PAIR_W, bf16), (PAD_W, bf16),
        (PAIR_W, bf16), (PAIR_W, f32), (PAIR_W, bf16), (PAIR_W, f32),
        (PAD_W, bf16), (PAD_W, bf16), (PAD_W, bf16),
    ]
    out_shape, out_specs = [], []
    for d in out_defs:
        out_shape.append(jax.ShapeDtypeStruct((B, S, d[0]), d[1]))
        out_specs.append(row(d[0]))
    tabs = list(ret_tabs) + list(mla_tabs)
    tail = [qn, kvn, wuq, wuk, wuv]
    return pl.pallas_call(
        functools.partial(_inproj_kernel, mla_scale=(MLA_NOPE + MLA_ROPE) ** -0.5 * LOG2E),
        grid=(B, ns),
        in_specs=[row(D), full(g), full(w_in), full(bf)] + [tab] * 6 + [full(a) for a in tail],
        out_specs=out_specs,
        out_shape=out_shape,
        scratch_shapes=[pltpu.VMEM((1, LANES), f32), pltpu.VMEM((IN_CAT_W, D), bf16)],
        compiler_params=pltpu.CompilerParams(
            dimension_semantics=("arbitrary", "arbitrary"), vmem_limit_bytes=VMEM_LIMIT),
        name="inproj",
    )(x, g, w_in, bf, *tabs, *tail)


def _head_q(q_ref, h, pair_q):
    if pair_q:
        g, e = divmod(h, 2)
        q2 = q_ref[:, g * LANES:(g + 1) * LANES]
        keep = (_lane(q2.shape) >= HEAD_DIM) if e else (_lane(q2.shape) < HEAD_DIM)
        return jnp.where(keep, q2, jnp.zeros_like(q2)), g
    return q_ref[:, h * LANES:(h + 1) * LANES], h


def _pack_pair(o_even, o_odd):
    return jnp.where(_lane(o_even.shape) < HEAD_DIM, o_even, pltpu.roll(o_odd, HEAD_DIM, axis=1))


def _flash_kernel(q_ref, k_ref, va_ref, o_ref, qop_ref, m_ref, acc_ref, s0_ref, *, fox, tk):
    tq = q_ref.shape[0]
    assert tq == 2 * tk
    qi = pl.program_id(1)
    for h in range(N_HEADS):
        if fox:
            qm, _ = _head_q(q_ref, h, True)
            lane = _lane(qm.shape)
            ones = jnp.where((lane % N_HEADS == h) & (lane < 3 * N_HEADS), 1.0, 0.0).astype(qm.dtype)
            qop_ref[h] = jnp.concatenate([qm, ones], axis=1)
        else:
            qop_ref[h] = q_ref[:, h * LANES:(h + 1) * LANES]
    m_ref[...] = jnp.full(m_ref.shape, NEG_BIG, jnp.float32)
    acc_ref[...] = jnp.zeros(acc_ref.shape, jnp.float32)

    def keys(j, h):
        start = pl.multiple_of(j * tk, tk)
        if fox:
            g = h // 2
            return jnp.concatenate([k_ref[pl.ds(start, tk), g * LANES:(g + 1) * LANES],
                                    k_ref[pl.ds(start, tk), PAIR_W:PAIR_W + LANES]], axis=1)
        return k_ref[pl.ds(start, tk), h * LANES:(h + 1) * LANES]

    def scores(h, j, rows):
        return _mm_nt(qop_ref[h, rows, :], keys(j, h))

    def step(j, rows, causal, next_rows):
        n_rows = rows.stop - rows.start
        start = pl.multiple_of(j * tk, tk)
        s_next = s0_ref[0:n_rows, :]
        for h in range(N_HEADS):
            s = s_next
            if h + 1 < N_HEADS:
                s_next = scores(h + 1, j, rows)
            elif next_rows is not None:
                s0_ref[0:next_rows.stop - next_rows.start, :] = scores(0, j + 1, next_rows)
            if causal:
                r = lax.broadcasted_iota(jnp.int32, s.shape, 0)
                c = lax.broadcasted_iota(jnp.int32, s.shape, 1)
                s = jnp.where(c <= r, s, NEG_BIG)
            m = m_ref[h, rows, :]
            m_new = jnp.maximum(m, jnp.max(s, axis=-1, keepdims=True))
            p = jnp.concatenate([jnp.exp2(s[:, c * LANES:(c + 1) * LANES] - m_new) for c in range(tk // LANES)],
                                axis=1).astype(MXU_DTYPE)
            acc_ref[h, rows, :] = (jnp.exp2(m - m_new) * acc_ref[h, rows, :]
                                   + _mm(p, va_ref[pl.ds(start, tk), h * LANES:(h + 1) * LANES]))
            m_ref[h, rows, :] = m_new

    every, lower = slice(0, tq), slice(tk, tq)

    def body(j, carry):
        step(j, every, False, every)
        return carry

    s0_ref[...] = scores(0, 0, every)
    lax.fori_loop(0, 2 * qi, body, 0)
    step(2 * qi, every, True, lower)
    step(2 * qi + 1, lower, True, None)
    outs = []
    for h in range(N_HEADS):
        acc = acc_ref[h]
        outs.append(acc / pltpu.roll(acc, HEAD_DIM, axis=1))
    for g in range(N_HEADS // 2):
        o_ref[:, g * LANES:(g + 1) * LANES] = _pack_pair(outs[2 * g], outs[2 * g + 1]).astype(o_ref.dtype)


def _flash(q, k, va, fox):
    B, S, qw = q.shape
    tk = ROW_TILE
    tq = 2 * tk
    seq = lambda w: pl.BlockSpec((None, S, w), lambda b, i: (b, 0, 0))
    return pl.pallas_call(
        functools.partial(_flash_kernel, fox=fox, tk=tk),
        grid=(B, S // tq),
        in_specs=[pl.BlockSpec((None, tq, qw), lambda b, i: (b, i, 0)), seq(k.shape[-1]), seq(PAD_W)],
        out_specs=pl.BlockSpec((None, tq, PAIR_W), lambda b, i: (b, i, 0)),
        out_shape=jax.ShapeDtypeStruct((B, S, PAIR_W), MXU_DTYPE),
        scratch_shapes=[pltpu.VMEM((N_HEADS, tq, 2 * LANES if fox else LANES), MXU_DTYPE),
                        pltpu.VMEM((N_HEADS, tq, LANES), jnp.float32),
                        pltpu.VMEM((N_HEADS, tq, LANES), jnp.float32),
                        pltpu.VMEM((tq, tk), jnp.float32)],
        compiler_params=pltpu.CompilerParams(
            dimension_semantics=("arbitrary", "arbitrary"), vmem_limit_bytes=VMEM_LIMIT),
        name="fox_attn" if fox else "mla_attn",
    )(q, k, va)


def _dilated_kernel(q_ref, k_ref, va_ref, o_ref, qf_ref, kf_ref, vf_ref, acc_ref, stage_ref):
    S = q_ref.shape[0]
    n = DIL_BLOCK
    tile = ROW_TILE
    quarter = S // 4
    run = n // 4
    keep_stats = _lane((n, LANES)) < HEAD_DIM + HEAD_DIM // 2

    def widen(i, carry):
        rows = pl.ds(pl.multiple_of(i * tile, tile), tile)
        slot = 0
        for src, dst in ((q_ref, qf_ref), (k_ref, kf_ref), (va_ref, vf_ref)):
            for c in range(dst.shape[0]):
                stage_ref[slot] = _f32(src[rows, c * LANES:(c + 1) * LANES])
                for cls in range(4):
                    dst[c, pl.ds(cls * quarter + i * (tile // 4), tile // 4), :] = (
                        stage_ref[slot, pl.ds(cls, tile // 4, stride=4), :])
                slot += 1
        return carry

    lax.fori_loop(0, S // tile, widen, 0)

    ri = lax.broadcasted_iota(jnp.int32, (n, 2 * n), 0)
    ci = lax.broadcasted_iota(jnp.int32, (n, 2 * n), 1)
    diff_class = ri - ci
    diff_runs = (4 * (jnp.bitwise_and(ri, run - 1) - jnp.bitwise_and(ci, 2 * run - 1))
                 + jnp.right_shift(ri, run.bit_length() - 1) - jnp.right_shift(ci, (2 * run).bit_length() - 1))

    def plan(dil, u):
        if dil == 1:
            first = u == 0
            q_runs = [(cls * quarter + run * u, run, 1) for cls in range(4)]
            w_runs = [(cls * quarter + run * jnp.maximum(u - 1, 0), 2 * run, 1) for cls in range(4)]
            return q_runs, w_runs, diff_runs, first
        if dil == 4:
            cls, a = jnp.bitwise_and(u, 3), jnp.right_shift(u, 2)
            base, step, stride = cls * quarter, n, 1
        else:
            rho, a = jnp.bitwise_and(u, 15), jnp.right_shift(u, 4)
            base, step, stride = jnp.bitwise_and(rho, 3) * quarter + jnp.right_shift(rho, 2), 4 * n, 4
        q_runs = [(base + step * a, n, stride)]
        w_runs = [(base + step * jnp.maximum(a - 1, 0), 2 * n, stride)]
        return q_runs, w_runs, diff_class, a == 0

    def rows_of(r):
        start, size, stride = r
        return pl.ds(start, size, stride=stride) if stride > 1 else pl.ds(start, size)

    def load(ref, idx, runs):
        parts = [ref[idx, rows_of(r), :] for r in runs]
        return parts[0] if len(parts) == 1 else jnp.concatenate(parts, axis=0)

    def store(ref, idx, runs, val):
        at = 0
        for r in runs:
            ref[idx, rows_of(r), :] = val[at:at + r[1], :]
            at += r[1]

    for number, (_, dil) in enumerate(DIL_PATTERNS):

        def blocks(t, carry, dil=dil, merge=number > 0):
            todo = []
            for sub in range(DIL_UNROLL):
                q_runs, w_runs, diff, first = plan(dil, t * DIL_UNROLL + sub)
                dist = diff + jnp.where(first, 0, n)
                valid = (dist >= 0) & (dist <= n)
                q2 = [load(qf_ref, g, q_runs).astype(MXU_DTYPE) for g in range(N_HEADS // 2)]
                k2 = [load(kf_ref, g, w_runs).astype(MXU_DTYPE) for g in range(N_HEADS // 2)]
                v4 = [load(vf_ref, h, w_runs).astype(MXU_DTYPE) for h in range(N_HEADS)]
                old = [load(acc_ref, h, q_runs) for h in range(N_HEADS)] if merge else None
                m_old = [jnp.broadcast_to(o[:, LANES - 1:LANES], (n, LANES)) for o in old] if merge else None
                todo.append((q_runs, valid, q2, k2, v4, old, m_old))
            done = []
            for q_runs, valid, q2, k2, v4, old, m_old in todo:
                for h in range(N_HEADS):
                    g, e = divmod(h, 2)
                    keep = (_lane(q2[g].shape) >= HEAD_DIM) if e else (_lane(q2[g].shape) < HEAD_DIM)
                    qh = jnp.where(keep, q2[g], jnp.zeros_like(q2[g]))
                    s = jnp.where(valid, _mm_nt(qh, k2[g]), NEG_BIG)
                    m_new = jnp.max(s, axis=-1, keepdims=True)
                    if merge:
                        m_new = jnp.maximum(m_old[h], m_new)
                    p = jnp.concatenate([jnp.exp2(s[:, c * LANES:(c + 1) * LANES] - m_new) for c in range(2)],
                                        axis=1).astype(MXU_DTYPE)
                    upd = _mm(p, v4[h])
                    if merge:
                        upd = jnp.exp2(m_old[h] - m_new) * old[h] + upd
                    done.append((h, q_runs, jnp.where(keep_stats, upd, m_new)))
            for h, q_runs, val in done:
                store(acc_ref, h, q_runs, val)
            return carry

        lax.fori_loop(0, S // n // DIL_UNROLL, blocks, 0)

    def finish(i, carry):
        rows = pl.ds(pl.multiple_of(i * tile, tile), tile)
        outs = []
        for h in range(N_HEADS):
            for cls in range(4):
                a_h = acc_ref[h, pl.ds(cls * quarter + i * (tile // 4), tile // 4), :]
                stage_ref[h, pl.ds(cls, tile // 4, stride=4), :] = a_h / a_h[:, HEAD_DIM:HEAD_DIM + 1]
            outs.append(stage_ref[h])
        for g in range(N_HEADS // 2):
            o_ref[rows, g * LANES:(g + 1) * LANES] = _pack_pair(outs[2 * g], outs[2 * g + 1]).astype(o_ref.dtype)
        return carry

    lax.fori_loop(0, S // tile, finish, 0)


def _dilated(q, k, va):
    B, S, _ = q.shape
    assert S % (DIL_PATTERNS[-1][1] * 2 * DIL_BLOCK) == 0 and S % ROW_TILE == 0
    seq = lambda w: pl.BlockSpec((None, S, w), lambda b: (b, 0, 0))
    f32 = jnp.float32
    n_chunks = (2 * PAIR_W + PAD_W) // LANES
    return pl.pallas_call(
        _dilated_kernel,
        grid=(B,),
        in_specs=[seq(PAIR_W), seq(PAIR_W), seq(PAD_W)],
        out_specs=seq(PAIR_W),
        out_shape=jax.ShapeDtypeStruct((B, S, PAIR_W), MXU_DTYPE),
        scratch_shapes=[pltpu.VMEM((PAIR_W // LANES, S, LANES), f32), pltpu.VMEM((PAIR_W // LANES, S, LANES), f32),
                        pltpu.VMEM((N_HEADS, S, LANES), f32), pltpu.VMEM((N_HEADS, S, LANES), f32),
                        pltpu.VMEM((n_chunks, ROW_TILE, LANES), f32)],
        compiler_params=pltpu.CompilerParams(dimension_semantics=("arbitrary",), vmem_limit_bytes=VMEM_LIMIT),
        name="dilated",
    )(q, k, va)


_RET_LOG_G = [math.log1p(-(2.0 ** (-5.0 - h))) for h in range(N_HEADS)]


def _retention_tables(T, decay_ref, xi_ref, zeta_ref):
    lane = _lane((T, LANES))
    low = lane < HEAD_DIM
    pos = _f32(lax.broadcasted_iota(jnp.int32, (T, LANES), 0))
    for c in range(T // LANES):
        rel = pos - _f32(lane) - float(c * LANES)
        for h in range(N_HEADS):
            decay_ref[h, :, c * LANES:(c + 1) * LANES] = jnp.where(
                rel >= 0, jnp.exp(jnp.maximum(rel, 0.0) * _RET_LOG_G[h]), 0.0)
    for g in range(N_HEADS // 2):
        lg_lane = jnp.where(low, _RET_LOG_G[2 * g], _RET_LOG_G[2 * g + 1])
        xi_ref[g] = jnp.exp((pos + 1.0) * lg_lane)
        zeta_ref[g] = jnp.exp((T - 1.0 - pos) * lg_lane)


def _retention_scores(g, q2, kf, v2, state_ref, decay_ref, xi_ref, zeta_ref):
    T = q2.shape[0]
    low = _lane((T, LANES)) < HEAD_DIM
    row_low = lax.broadcasted_iota(jnp.int32, (LANES, LANES), 0) < HEAD_DIM
    same_head = row_low == (_lane((LANES, LANES)) < HEAD_DIM)
    k2 = kf.astype(MXU_DTYPE)
    R = state_ref[g]
    cross = _mm(q2, R.astype(MXU_DTYPE)) * xi_ref[g]
    kv = _mm_tn((kf * zeta_ref[g]).astype(MXU_DTYPE), v2)
    g_rows = jnp.where(row_low, math.exp(T * _RET_LOG_G[2 * g]), math.exp(T * _RET_LOG_G[2 * g + 1]))
    state_ref[g] = g_rows * R + jnp.where(same_head, kv, 0.0)
    scores = []
    for e in range(2):
        qm = jnp.where(low if e == 0 else ~low, q2, jnp.zeros_like(q2))
        scores.append((_mm_nt(qm, k2) * decay_ref[2 * g + e]).astype(MXU_DTYPE))
    return scores, cross


def _retention_output(scores, cross, v2, sg, gain):
    low = _lane(cross.shape) < HEAD_DIM
    out = jnp.where(low, _mm(scores[0], v2), _mm(scores[1], v2)) + cross
    mu = jnp.where(low,
                   jnp.sum(jnp.where(low, out, 0.0), axis=-1, keepdims=True),
                   jnp.sum(jnp.where(low, 0.0, out), axis=-1, keepdims=True)) * (1.0 / HEAD_DIM)
    dlt = out - mu
    sq = dlt * dlt
    var = jnp.where(low,
                    jnp.sum(jnp.where(low, sq, 0.0), axis=-1, keepdims=True),
                    jnp.sum(jnp.where(low, 0.0, sq), axis=-1, keepdims=True)) * (1.0 / HEAD_DIM)
    return (sg * (dlt * lax.rsqrt(var + GN_EPS) * gain)).astype(MXU_DTYPE)


def _merge_kernel(x_ref, gpre_ref, oa_ref, ob_ref, od_ref, rq_ref, rk_ref, rv_ref, sg_ref, gain_ref,
                  wg_ref, wb_ref, wo_ref, gpost_ref, y_ref, state_ref, decay_ref, xi_ref, zeta_ref,
                  *, tiles_per_seq):
    T, D = x_ref.shape

    @pl.when(pl.program_id(0) == 0)
    def _():
        _retention_tables(T, decay_ref, xi_ref, zeta_ref)

    @pl.when(pl.program_id(0) % tiles_per_seq == 0)
    def _():
        state_ref[...] = jnp.zeros_like(state_ref)

    half = T // 2
    rows = [slice(0, half), slice(half, 2 * half)]
    xs = [x_ref[r, :] for r in rows]
    hs = [_rms(x, gpre_ref[...]).astype(MXU_DTYPE) for x in xs]
    twice_merged = [None, None]

    def add_branch(n, branch):
        for k, (r, h) in enumerate(zip(rows, hs)):
            t = jnp.tanh(_mm(h, wg_ref[:, n * D:(n + 1) * D]))
            proj = _mm(branch[r, :], wb_ref[n])
            term = proj * t + proj
            twice_merged[k] = term if twice_merged[k] is None else twice_merged[k] + term

    pair_cols = [slice(g * LANES, (g + 1) * LANES) for g in range(N_HEADS // 2)]
    vals = [rv_ref[:, c] for c in pair_cols]
    ret = [_retention_scores(g, rq_ref[:, c], rk_ref[:, c], vals[g], state_ref, decay_ref, xi_ref, zeta_ref)
           for g, c in enumerate(pair_cols)]
    add_branch(0, oa_ref)
    add_branch(1, ob_ref)
    o_c = jnp.concatenate(
        [_retention_output(ret[g][0], ret[g][1], vals[g], sg_ref[:, c], gain_ref[:, c])
         for g, c in enumerate(pair_cols)], axis=1)
    add_branch(3, od_ref)
    add_branch(2, o_c)
    for r, x, tm2 in zip(rows, xs, twice_merged):
        mix = _mm(tm2.astype(MXU_DTYPE), wo_ref[...])
        y_ref[r, :] = x + _rms(mix, gpost_ref[...])


def _merge(x2, l, tiles_per_seq, gpre, oa, ob, od, rq, rk, rv, sg, gain, wg_half, wb, wo_half, gpost):
    N, D = x2.shape
    tm = ROW_TILE
    f32 = jnp.float32
    row = lambda w: pl.BlockSpec((tm, w), lambda i: (i, 0))
    full = lambda a: pl.BlockSpec(a.shape, lambda i: (0,) * a.ndim, pipeline_mode=pl.Buffered(1))
    wg, wo = wg_half, wo_half
    return pl.pallas_call(
        functools.partial(_merge_kernel, tiles_per_seq=tiles_per_seq),
        grid=(N // tm,),
        in_specs=[row(D), full(gpre), row(PAIR_W), row(PAIR_W), row(PAIR_W),
                  row(PAIR_W), row(PAIR_W), row(PAIR_W), row(PAIR_W), full(gain),
                  _layer(wg, l, 1), _layer(wb, l, 1), _layer(wo, l, 1), full(gpost)],
        out_specs=row(D),
        out_shape=jax.ShapeDtypeStruct((N, D), jnp.float32),
        scratch_shapes=[pltpu.VMEM((N_HEADS // 2, LANES, LANES), f32), pltpu.VMEM((N_HEADS, tm, tm), f32),
                        pltpu.VMEM((N_HEADS // 2, tm, LANES), f32), pltpu.VMEM((N_HEADS // 2, tm, LANES), f32)],
        compiler_params=pltpu.CompilerParams(dimension_semantics=("arbitrary",), vmem_limit_bytes=VMEM_LIMIT),
        name="merge",
    )(x2, gpre, oa, ob, od, rq, rk, rv, sg, gain, wg, wb, wo, gpost)


def _ffn_kernel(x_ref, gpre_ref, wgate_ref, wup_ref, wdown_ref, gpost_ref, y_ref):
    half = x_ref.shape[0] // 2
    rows = [slice(0, half), slice(half, 2 * half)]
    xs = [x_ref[r, :] for r in rows]
    hs = [_rms(x, gpre_ref[...]).astype(MXU_DTYPE) for x in xs]
    acts = []
    for h in hs:
        gate = _mm(h, wgate_ref[...])
        acts.append((gate / (1.0 + jnp.exp(-gate)) * _mm(h, wup_ref[...])).astype(MXU_DTYPE))
    for r, x, act in zip(rows, xs, acts):
        y_ref[r, :] = x + _rms(_mm(act, wdown_ref[...]), gpost_ref[...])


def _ffn(x2, l, gpre, wgate, wup, wdown, gpost):
    N, D = x2.shape
    tm = ROW_TILE
    row = pl.BlockSpec((tm, D), lambda i: (i, 0))
    full = lambda a: pl.BlockSpec(a.shape, lambda i: (0,) * a.ndim, pipeline_mode=pl.Buffered(1))
    return pl.pallas_call(
        _ffn_kernel,
        grid=(N // tm,),
        in_specs=[row, full(gpre), _layer(wgate, l, 1), _layer(wup, l, 1), _layer(wdown, l, 1), full(gpost)],
        out_specs=row,
        out_shape=jax.ShapeDtypeStruct((N, D), jnp.float32),
        compiler_params=pltpu.CompilerParams(dimension_semantics=("arbitrary",), vmem_limit_bytes=VMEM_LIMIT),
        name="ffn",
    )(x2, gpre, wgate, wup, wdown, gpost)


def _rope_tables(S, half, lo):
    period = 2 * half if lo == 0 else LANES
    inv = ROPE_THETA ** (-jnp.arange(half, dtype=jnp.float32) / half)
    ang = jnp.arange(S, dtype=jnp.float32)[:, None] * inv[None, :]
    cos, sin = jnp.cos(ang), jnp.sin(ang)
    reps = LANES // period
    pad_lo = jnp.zeros((S, lo), jnp.float32)
    pad_hi = jnp.zeros((S, period - lo - 2 * half), jnp.float32)
    zeros = jnp.zeros((S, half), jnp.float32)
    cos_t = jnp.concatenate([pad_lo + 1.0, cos, cos, pad_hi + 1.0], axis=1)
    sin_up = jnp.concatenate([pad_lo, -sin, zeros, pad_hi], axis=1)
    sin_dn = jnp.concatenate([pad_lo, zeros, sin, pad_hi], axis=1)
    return tuple(jnp.tile(t, (1, reps)) for t in (cos_t, sin_up, sin_dn))


def _pad_heads(w, width=HEAD_DIM):
    K = w.shape[0]
    w = w.reshape(K, N_HEADS, width)
    return jnp.pad(w, ((0, 0), (0, 0), (0, LANES - width))).reshape(K, PAD_W)


def _layer_weights(b_forget, w_uq, w_ukv):
    bf = jnp.pad(b_forget, (0, LANES - N_HEADS)).reshape(1, LANES)
    wuq = _pad_heads(w_uq, MLA_NOPE + MLA_ROPE).astype(MXU_DTYPE)
    ukv = w_ukv.reshape(MLA_KV_RANK, N_HEADS, MLA_NOPE + MLA_V)
    wuk = _pad_heads(ukv[:, :, :MLA_NOPE].reshape(MLA_KV_RANK, -1)).astype(MXU_DTYPE)
    wuv = _pad_heads(ukv[:, :, MLA_NOPE:].reshape(MLA_KV_RANK, -1)).astype(MXU_DTYPE)
    return bf, wuq, wuk, wuv


def kernel(x, w_in, b_forget, ret_gn_gain, mla_q_norm, mla_kv_norm, w_uq, w_ukv, w_gate, w_branch, w_out,
           g_pre_mix, g_post_mix, g_pre_ffn, g_post_ffn, w_ffn_gate, w_ffn_up, w_ffn_down):
    B, S, D = x.shape
    depth = w_in.shape[0]
    assert S % ROW_TILE == 0
    ret_tabs = _rope_tables(S, HEAD_DIM // 2, 0)
    mla_tabs = _rope_tables(S, MLA_ROPE // 2, MLA_NOPE)
    r1 = lambda v: v.reshape(1, -1)
    assert w_in.shape[2] == IN_KR_COL + MLA_ROPE
    wg_half = (0.5 * w_gate).astype(MXU_DTYPE)
    wo_half = (0.5 * w_out).astype(MXU_DTYPE)
    wb = w_branch.astype(MXU_DTYPE)
    wf_gate, wf_up, wf_down = (w.astype(MXU_DTYPE) for w in (w_ffn_gate, w_ffn_up, w_ffn_down))
    for l in range(depth):
        bf, wuq, wuk, wuv = _layer_weights(b_forget[l], w_uq[l], w_ukv[l])
        (fq, fk, fva, dq, dk, dva, rq, rk, rv, sg, mq, mk, mva) = _inproj(
            x, r1(g_pre_mix[l]), w_in[l].T, bf, ret_tabs, mla_tabs,
            r1(mla_q_norm[l]), r1(mla_kv_norm[l]), wuq, wuk, wuv)
        o_a = _flash(fq, fk, fva, True)
        o_b = _dilated(dq, dk, dva)
        o_d = _flash(mq, mk, mva, False)
        flat = lambda a: a.reshape(B * S, a.shape[-1])
        x2 = _merge(flat(x), l, S // ROW_TILE, r1(g_pre_mix[l]), flat(o_a), flat(o_b), flat(o_d),
                    flat(rq), flat(rk), flat(rv), flat(sg), r1(ret_gn_gain[l]),
                    wg_half, wb, wo_half, r1(g_post_mix[l]))
        x2 = _ffn(x2, l, r1(g_pre_ffn[l]), wf_gate, wf_up, wf_down, r1(g_post_ffn[l]))
        x = x2.reshape(B, S, D)
    return x
```

```python
import functools
import math

import jax
import jax.numpy as jnp
from jax import lax
from jax.experimental import pallas as pl
from jax.experimental.pallas import tpu as pltpu

HEAD_DIM = 64
N_HEADS = 4
PAIR_W = N_HEADS * HEAD_DIM
LANES = 128
PAD_W = N_HEADS * LANES
DIL_PATTERNS = ((128, 1), (512, 4), (2048, 16))
DIL_BLOCK = 128
DIL_UNROLL = 8
FLASH_Q_BLOCKS = 4
MLA_Q_RANK = 256
MLA_KV_RANK = 128
MLA_NOPE = 64
MLA_ROPE = 32
MLA_V = 64
ROPE_THETA = 10000.0
RMS_EPS = 1e-6
GN_EPS = 1e-5
NEG_BIG = -1e30
LOG2E = math.log2(math.e)

MXU_DTYPE = jnp.bfloat16
ROW_TILE = 512
VMEM_LIMIT = 56 * 1024 * 1024

_C_FQ, _C_FK, _C_FV = 0, 256, 512
_C_DQ, _C_DK, _C_DV = 768, 1024, 1280
_C_RQ, _C_RK, _C_RV, _C_RG = 1536, 1792, 2048, 2304
_C_CQ, _C_TAIL = 2560, 2816
IN_CAT_W = 3072
IN_FF_COL = 3 * PAIR_W
IN_KR_COL = 2948


def _f32(x):
    return x.astype(jnp.float32)


def _rms(x, g):
    ms = jnp.mean(x * x, axis=-1, keepdims=True)
    return x * lax.rsqrt(ms + RMS_EPS) * g


def _mm(a, b):
    return jnp.dot(a, b, preferred_element_type=jnp.float32)


def _mm_nt(a, b):
    return lax.dot_general(a, b, (((1,), (1,)), ((), ())), preferred_element_type=jnp.float32)


def _mm_tn(a, b):
    return lax.dot_general(a, b, (((0,), (0,)), ((), ())), preferred_element_type=jnp.float32)


def _lane(shape):
    return lax.broadcasted_iota(jnp.int32, shape, len(shape) - 1)


def _rope_chunks(a, cos, sin_a, sin_b, half):
    outs = []
    for c in range(a.shape[1] // LANES):
        t = a[:, c * LANES:(c + 1) * LANES]
        up = pltpu.roll(t, LANES - half, axis=1)
        dn = pltpu.roll(t, half, axis=1)
        outs.append(t * cos + up * sin_a + dn * sin_b)
    return outs[0] if len(outs) == 1 else jnp.concatenate(outs, axis=1)


def _inproj_kernel(x_ref, g_ref, w_ref, bf_ref, rcos_ref, rsa_ref, rsb_ref, mcos_ref, msa_ref, msb_ref,
                   qn_ref, kvn_ref, wuq_ref, wuk_ref, wuv_ref,
                   fq_ref, fk_ref, fva_ref,
                   dq_ref, dk_ref, dva_ref,
                   rq_ref, rk_ref, rv_ref, sg_ref,
                   mq_ref, mk_ref, mva_ref,
                   carry_ref, wb_ref, *, mla_scale):
    tm = x_ref.shape[0]

    @pl.when((pl.program_id(0) == 0) & (pl.program_id(1) == 0))
    def _():
        n_ff = N_HEADS
        for r in range(0, IN_FF_COL, PAIR_W):
            wb_ref[r:r + PAIR_W, :] = w_ref[r:r + PAIR_W, :].astype(wb_ref.dtype)
        for r in range(IN_FF_COL, _C_TAIL + LANES, LANES):
            wb_ref[r:r + LANES, :] = w_ref[r + n_ff:r + n_ff + LANES, :].astype(wb_ref.dtype)
        blk = _C_TAIL + LANES
        wb_ref[blk:blk + LANES, :] = jnp.zeros((LANES, wb_ref.shape[1]), wb_ref.dtype)
        head = w_ref[IN_FF_COL:IN_FF_COL + 2 * n_ff, :]
        is_ff = lax.broadcasted_iota(jnp.int32, head.shape, 0) < n_ff
        wb_ref[blk:blk + 2 * n_ff, :] = jnp.where(is_ff, head, 0.0).astype(wb_ref.dtype)
        wb_ref[blk + MLA_NOPE:blk + MLA_NOPE + MLA_ROPE, :] = (
            w_ref[IN_KR_COL:IN_KR_COL + MLA_ROPE, :].astype(wb_ref.dtype))

    @pl.when(pl.program_id(1) == 0)
    def _():
        carry_ref[...] = jnp.zeros_like(carry_ref)

    h = _rms(x_ref[...], g_ref[...]).astype(MXU_DTYPE)

    def proj(lo, width):
        return _mm_nt(h, wb_ref[lo:lo + width, :])

    ones_up = jnp.where(_lane((1, PAD_W)) % LANES >= HEAD_DIM, 1.0, 0.0)
    mcos, msa, msb = mcos_ref[...], msa_ref[...], msb_ref[...]
    rcos, rsa, rsb = rcos_ref[...], rsa_ref[...], rsb_ref[...]

    def with_ones(v):
        out = []
        for hd in range(N_HEADS):
            chunk = v[:, (hd // 2) * LANES:(hd // 2 + 1) * LANES]
            if hd % 2:
                chunk = pltpu.roll(chunk, HEAD_DIM, axis=1)
            out.append(jnp.where(_lane(chunk.shape) < HEAD_DIM, chunk, 1.0))
        return jnp.concatenate(out, axis=1)


    tail = proj(_C_TAIL, 2 * LANES)
    cq_raw = proj(_C_CQ, MLA_Q_RANK)
    ffkr = tail[:, LANES:2 * LANES]
    ff = ffkr + bf_ref[...]
    log_f = jnp.minimum(ff, 0.0) - jnp.log1p(jnp.exp(-jnp.abs(ff)))
    rows = lax.broadcasted_iota(jnp.int32, (tm, LANES), 0)
    cs = log_f
    step = 1
    while step < tm:
        cs = cs + jnp.where(rows >= step, pltpu.roll(cs, step, axis=0), 0.0)
        step *= 2
    cs = cs + carry_ref[...]
    carry_ref[...] = cs[tm - 1:tm, :]
    cs2 = cs * LOG2E
    hi = _f32(cs2.astype(MXU_DTYPE))
    rem = cs2 - hi
    mid = _f32(rem.astype(MXU_DTYPE))
    lo = rem - mid
    lane = _lane(cs2.shape)
    split = jnp.where(lane < N_HEADS, hi,
                      jnp.where(lane < 2 * N_HEADS, pltpu.roll(mid, N_HEADS, axis=1),
                                jnp.where(lane < 3 * N_HEADS, pltpu.roll(lo, 2 * N_HEADS, axis=1), 0.0)))
    fk_ref[:, PAIR_W:PAIR_W + LANES] = (-split).astype(fk_ref.dtype)

    rq_ref[...] = _rope_chunks(proj(_C_RQ, PAIR_W), rcos, rsa, rsb, HEAD_DIM // 2).astype(rq_ref.dtype)
    rk_ref[...] = _rope_chunks(proj(_C_RK, PAIR_W), rcos, rsa, rsb, HEAD_DIM // 2) * HEAD_DIM ** -0.5
    rg = proj(_C_RG, PAIR_W)
    sg_ref[...] = rg / (1.0 + jnp.exp(-rg))

    ckv = _rms(tail[:, 0:LANES], kvn_ref[...]).astype(MXU_DTYPE)
    rope_lanes = (_lane(ffkr.shape) >= MLA_NOPE) & (_lane(ffkr.shape) < MLA_NOPE + MLA_ROPE)
    kr = jnp.where(rope_lanes, _rope_chunks(ffkr, mcos, msa, msb, MLA_ROPE // 2), 0.0)
    mk_ref[...] = (_mm(ckv, wuk_ref[...]) + jnp.concatenate([kr] * N_HEADS, axis=1)).astype(mk_ref.dtype)
    mva_ref[...] = (_mm(ckv, wuv_ref[...]) + ones_up).astype(mva_ref.dtype)
    cq = _rms(cq_raw, qn_ref[...]).astype(MXU_DTYPE)
    q = _rope_chunks(_mm(cq, wuq_ref[...]), mcos, msa, msb, MLA_ROPE // 2)
    mq_ref[...] = (q * mla_scale).astype(mq_ref.dtype)

    fva_ref[...] = with_ones(proj(_C_FV, PAIR_W)).astype(fva_ref.dtype)
    dva_ref[...] = with_ones(proj(_C_DV, PAIR_W)).astype(dva_ref.dtype)
    fq_ref[...] = (proj(_C_FQ, PAIR_W) * (HEAD_DIM ** -0.5 * LOG2E)).astype(fq_ref.dtype)
    dq_ref[...] = (proj(_C_DQ, PAIR_W) * (HEAD_DIM ** -0.5 * LOG2E)).astype(dq_ref.dtype)

    fk_ref[:, 0:PAIR_W] = proj(_C_FK, PAIR_W).astype(fk_ref.dtype)
    dk_ref[...] = proj(_C_DK, PAIR_W).astype(dk_ref.dtype)
    rv_ref[...] = proj(_C_RV, PAIR_W).astype(rv_ref.dtype)


def _layer(a, l, n_grid):
    zeros = (0,) * (a.ndim - 1)
    index = (lambda b, i: (l,) + zeros) if n_grid == 2 else (lambda i: (l,) + zeros)
    return pl.BlockSpec((None,) + a.shape[1:], index, pipeline_mode=pl.Buffered(1))


def _inproj(x, g, w_in, bf, ret_tabs, mla_tabs, qn, kvn, wuq, wuk, wuv):
    B, S, D = x.shape
    tm = ROW_TILE
    ns = S // tm
    row = lambda w: pl.BlockSpec((None, tm, w), lambda b, i: (b, i, 0))
    full = lambda a: pl.BlockSpec(a.shape, lambda b, i: (0,) * a.ndim, pipeline_mode=pl.Buffered(1))
    tab = pl.BlockSpec((tm, LANES), lambda b, i: (i, 0))
    bf16, f32 = MXU_DTYPE, jnp.float32
    out_defs = [
        (PAIR_W, bf16), (PAIR_W + LANES, bf16), (PAD_W, bf16),
        (PAIR_W, bf16), (PAIR_W, bf16), (PAD_W, bf16),
        (PAIR_W, bf16), (PAIR_W, f32), (PAIR_W, bf16), (PAIR_W, f32),
        (PAD_W, bf16), (PAD_W, bf16), (PAD_W, bf16),
    ]
    out_shape, out_specs = [], []
    for d in out_defs:
        out_shape.append(jax.ShapeDtypeStruct((B, S, d[0]), d[1]))
        out_specs.append(row(d[0]))
    tabs = list(ret_tabs) + list(mla_tabs)
    tail = [qn, kvn, wuq, wuk, wuv]
    return pl.pallas_call(
        functools.partial(_inproj_kernel, mla_scale=(MLA_NOPE + MLA_ROPE) ** -0.5 * LOG2E),
        grid=(B, ns),
        in_specs=[row(D), full(g), full(w_in), full(bf)] + [tab] * 6 + [full(a) for a in tail],
        out_specs=out_specs,
        out_shape=out_shape,
        scratch_shapes=[pltpu.VMEM((1, LANES), f32), pltpu.VMEM((IN_CAT_W, D), bf16)],
        compiler_params=pltpu.CompilerParams(
            dimension_semantics=("arbitrary", "arbitrary"), vmem_limit_bytes=VMEM_LIMIT),
        name="inproj",
    )(x, g, w_in, bf, *tabs, *tail)


def _head_q(q_ref, h, pair_q):
    if pair_q:
        g, e = divmod(h, 2)
        q2 = q_ref[:, g * LANES:(g + 1) * LANES]
        keep = (_lane(q2.shape) >= HEAD_DIM) if e else (_lane(q2.shape) < HEAD_DIM)
        return jnp.where(keep, q2, jnp.zeros_like(q2)), g
    return q_ref[:, h * LANES:(h + 1) * LANES], h


def _pack_pair(o_even, o_odd):
    return jnp.where(_lane(o_even.shape) < HEAD_DIM, o_even, pltpu.roll(o_odd, HEAD_DIM, axis=1))


def _flash_kernel(q_ref, k_ref, va_ref, o_ref, qop_ref, m_ref, acc_ref, s0_ref, *, fox, tk):
    tq = q_ref.shape[0]
    n_blk = tq // tk
    qi = pl.program_id(1)
    for h in range(N_HEADS):
        if fox:
            qm, _ = _head_q(q_ref, h, True)
            lane = _lane(qm.shape)
            ones = jnp.where((lane % N_HEADS == h) & (lane < 3 * N_HEADS), 1.0, 0.0).astype(qm.dtype)
            qop_ref[h] = jnp.concatenate([qm, ones], axis=1)
        else:
            qop_ref[h] = q_ref[:, h * LANES:(h + 1) * LANES]
    m_ref[...] = jnp.full(m_ref.shape, NEG_BIG, jnp.float32)
    acc_ref[...] = jnp.zeros(acc_ref.shape, jnp.float32)

    def keys(j, h):
        start = pl.multiple_of(j * tk, tk)
        if fox:
            g = h // 2
            return jnp.concatenate([k_ref[pl.ds(start, tk), g * LANES:(g + 1) * LANES],
                                    k_ref[pl.ds(start, tk), PAIR_W:PAIR_W + LANES]], axis=1)
        return k_ref[pl.ds(start, tk), h * LANES:(h + 1) * LANES]

    def scores(h, j, rows):
        return _mm_nt(qop_ref[h, rows, :], keys(j, h))

    def step(j, rows, causal, next_rows):
        n_rows = rows.stop - rows.start
        start = pl.multiple_of(j * tk, tk)
        s_next = s0_ref[0:n_rows, :]
        for h in range(N_HEADS):
            s = s_next
            if h + 1 < N_HEADS:
                s_next = scores(h + 1, j, rows)
            elif next_rows is not None:
                s0_ref[0:next_rows.stop - next_rows.start, :] = scores(0, j + 1, next_rows)
            if causal:
                r = lax.broadcasted_iota(jnp.int32, s.shape, 0)
                c = lax.broadcasted_iota(jnp.int32, s.shape, 1)
                s = jnp.where(c <= r, s, NEG_BIG)
            m = m_ref[h, rows, :]
            m_new = jnp.maximum(m, jnp.max(s, axis=-1, keepdims=True))
            p = jnp.concatenate([jnp.exp2(s[:, c * LANES:(c + 1) * LANES] - m_new) for c in range(tk // LANES)],
                                axis=1).astype(MXU_DTYPE)
            acc_ref[h, rows, :] = (jnp.exp2(m - m_new) * acc_ref[h, rows, :]
                                   + _mm(p, va_ref[pl.ds(start, tk), h * LANES:(h + 1) * LANES]))
            m_ref[h, rows, :] = m_new

    every = slice(0, tq)

    def body(j, carry):
        step(j, every, False, every)
        return carry

    s0_ref[...] = scores(0, 0, every)
    lax.fori_loop(0, n_blk * qi, body, 0)
    for d in range(n_blk):
        step(n_blk * qi + d, slice(d * tk, tq), True, slice((d + 1) * tk, tq) if d + 1 < n_blk else None)
    outs = []
    for h in range(N_HEADS):
        acc = acc_ref[h]
        outs.append(acc / pltpu.roll(acc, HEAD_DIM, axis=1))
    for g in range(N_HEADS // 2):
        o_ref[:, g * LANES:(g + 1) * LANES] = _pack_pair(outs[2 * g], outs[2 * g + 1]).astype(o_ref.dtype)


def _flash(q, k, va, fox):
    B, S, qw = q.shape
    tk = ROW_TILE
    tq = FLASH_Q_BLOCKS * tk
    assert S % tq == 0
    seq = lambda w: pl.BlockSpec((None, S, w), lambda b, i: (b, 0, 0))
    return pl.pallas_call(
        functools.partial(_flash_kernel, fox=fox, tk=tk),
        grid=(B, S // tq),
        in_specs=[pl.BlockSpec((None, tq, qw), lambda b, i: (b, i, 0)), seq(k.shape[-1]), seq(PAD_W)],
        out_specs=pl.BlockSpec((None, tq, PAIR_W), lambda b, i: (b, i, 0)),
        out_shape=jax.ShapeDtypeStruct((B, S, PAIR_W), MXU_DTYPE),
        scratch_shapes=[pltpu.VMEM((N_HEADS, tq, 2 * LANES if fox else LANES), MXU_DTYPE),
                        pltpu.VMEM((N_HEADS, tq, LANES), jnp.float32),
                        pltpu.VMEM((N_HEADS, tq, LANES), jnp.float32),
                        pltpu.VMEM((tq, tk), jnp.float32)],
        compiler_params=pltpu.CompilerParams(
            dimension_semantics=("arbitrary", "arbitrary"), vmem_limit_bytes=VMEM_LIMIT),
        name="fox_attn" if fox else "mla_attn",
    )(q, k, va)


def _dilated_kernel(q_ref, k_ref, va_ref, o_ref, qf_ref, kf_ref, vf_ref, acc_ref, stage_ref):
    S = q_ref.shape[0]
    n = DIL_BLOCK
    tile = ROW_TILE
    quarter = S // 4
    run = n // 4
    keep_stats = _lane((n, LANES)) < HEAD_DIM + HEAD_DIM // 2

    def widen(i, carry):
        rows = pl.ds(pl.multiple_of(i * tile, tile), tile)
        slot = 0
        for src, dst in ((q_ref, qf_ref), (k_ref, kf_ref), (va_ref, vf_ref)):
            for c in range(dst.shape[0]):
                stage_ref[slot] = _f32(src[rows, c * LANES:(c + 1) * LANES])
                for cls in range(4):
                    dst[c, pl.ds(cls * quarter + i * (tile // 4), tile // 4), :] = (
                        stage_ref[slot, pl.ds(cls, tile // 4, stride=4), :])
                slot += 1
        return carry

    lax.fori_loop(0, S // tile, widen, 0)

    ri = lax.broadcasted_iota(jnp.int32, (n, 2 * n), 0)
    ci = lax.broadcasted_iota(jnp.int32, (n, 2 * n), 1)
    diff_class = ri - ci
    diff_runs = (4 * (jnp.bitwise_and(ri, run - 1) - jnp.bitwise_and(ci, 2 * run - 1))
                 + jnp.right_shift(ri, run.bit_length() - 1) - jnp.right_shift(ci, (2 * run).bit_length() - 1))

    def plan(dil, u):
        if dil == 1:
            first = u == 0
            q_runs = [(cls * quarter + run * u, run, 1) for cls in range(4)]
            w_runs = [(cls * quarter + run * jnp.maximum(u - 1, 0), 2 * run, 1) for cls in range(4)]
            return q_runs, w_runs, diff_runs, first
        if dil == 4:
            cls, a = jnp.bitwise_and(u, 3), jnp.right_shift(u, 2)
            base, step, stride = cls * quarter, n, 1
        else:
            rho, a = jnp.bitwise_and(u, 15), jnp.right_shift(u, 4)
            base, step, stride = jnp.bitwise_and(rho, 3) * quarter + jnp.right_shift(rho, 2), 4 * n, 4
        q_runs = [(base + step * a, n, stride)]
        w_runs = [(base + step * jnp.maximum(a - 1, 0), 2 * n, stride)]
        return q_runs, w_runs, diff_class, a == 0

    def rows_of(r):
        start, size, stride = r
        return pl.ds(start, size, stride=stride) if stride > 1 else pl.ds(start, size)

    def load(ref, idx, runs):
        parts = [ref[idx, rows_of(r), :] for r in runs]
        return parts[0] if len(parts) == 1 else jnp.concatenate(parts, axis=0)

    def store(ref, idx, runs, val):
        at = 0
        for r in runs:
            ref[idx, rows_of(r), :] = val[at:at + r[1], :]
            at += r[1]

    for number, (_, dil) in enumerate(DIL_PATTERNS):

        def blocks(t, carry, dil=dil, merge=number > 0):
            todo = []
            for sub in range(DIL_UNROLL):
                q_runs, w_runs, diff, first = plan(dil, t * DIL_UNROLL + sub)
                dist = diff + jnp.where(first, 0, n)
                valid = (dist >= 0) & (dist <= n)
                q2 = [load(qf_ref, g, q_runs).astype(MXU_DTYPE) for g in range(N_HEADS // 2)]
                k2 = [load(kf_ref, g, w_runs).astype(MXU_DTYPE) for g in range(N_HEADS // 2)]
                v4 = [load(vf_ref, h, w_runs).astype(MXU_DTYPE) for h in range(N_HEADS)]
                old = [load(acc_ref, h, q_runs) for h in range(N_HEADS)] if merge else None
                m_old = [jnp.broadcast_to(o[:, LANES - 1:LANES], (n, LANES)) for o in old] if merge else None
                todo.append((q_runs, valid, q2, k2, v4, old, m_old))
            done = []
            for q_runs, valid, q2, k2, v4, old, m_old in todo:
                for h in range(N_HEADS):
                    g, e = divmod(h, 2)
                    keep = (_lane(q2[g].shape) >= HEAD_DIM) if e else (_lane(q2[g].shape) < HEAD_DIM)
                    qh = jnp.where(keep, q2[g], jnp.zeros_like(q2[g]))
                    s = jnp.where(valid, _mm_nt(qh, k2[g]), NEG_BIG)
                    m_new = jnp.max(s, axis=-1, keepdims=True)
                    if merge:
                        m_new = jnp.maximum(m_old[h], m_new)
                    p = jnp.concatenate([jnp.exp2(s[:, c * LANES:(c + 1) * LANES] - m_new) for c in range(2)],
                                        axis=1).astype(MXU_DTYPE)
                    upd = _mm(p, v4[h])
                    if merge:
                        upd = jnp.exp2(m_old[h] - m_new) * old[h] + upd
                    done.append((h, q_runs, jnp.where(keep_stats, upd, m_new)))
            for h, q_runs, val in done:
                store(acc_ref, h, q_runs, val)
            return carry

        lax.fori_loop(0, S // n // DIL_UNROLL, blocks, 0)

    def finish(i, carry):
        rows = pl.ds(pl.multiple_of(i * tile, tile), tile)
        outs = []
        for h in range(N_HEADS):
            for cls in range(4):
                a_h = acc_ref[h, pl.ds(cls * quarter + i * (tile // 4), tile // 4), :]
                stage_ref[h, pl.ds(cls, tile // 4, stride=4), :] = a_h / a_h[:, HEAD_DIM:HEAD_DIM + 1]
            outs.append(stage_ref[h])
        for g in range(N_HEADS // 2):
            o_ref[rows, g * LANES:(g + 1) * LANES] = _pack_pair(outs[2 * g], outs[2 * g + 1]).astype(o_ref.dtype)
        return carry

    lax.fori_loop(0, S // tile, finish, 0)


def _dilated(q, k, va):
    B, S, _ = q.shape
    assert S % (DIL_PATTERNS[-1][1] * 2 * DIL_BLOCK) == 0 and S % ROW_TILE == 0
    seq = lambda w: pl.BlockSpec((None, S, w), lambda b: (b, 0, 0))
    f32 = jnp.float32
    n_chunks = (2 * PAIR_W + PAD_W) // LANES
    return pl.pallas_call(
        _dilated_kernel,
        grid=(B,),
        in_specs=[seq(PAIR_W), seq(PAIR_W), seq(PAD_W)],
        out_specs=seq(PAIR_W),
        out_shape=jax.ShapeDtypeStruct((B, S, PAIR_W), MXU_DTYPE),
        scratch_shapes=[pltpu.VMEM((PAIR_W // LANES, S, LANES), f32), pltpu.VMEM((PAIR_W // LANES, S, LANES), f32),
                        pltpu.VMEM((N_HEADS, S, LANES), f32), pltpu.VMEM((N_HEADS, S, LANES), f32),
                        pltpu.VMEM((n_chunks, ROW_TILE, LANES), f32)],
        compiler_params=pltpu.CompilerParams(dimension_semantics=("arbitrary",), vmem_limit_bytes=VMEM_LIMIT),
        name="dilated",
    )(q, k, va)


_RET_LOG_G = [math.log1p(-(2.0 ** (-5.0 - h))) for h in range(N_HEADS)]


def _retention_tables(T, decay_ref, xi_ref, zeta_ref):
    lane = _lane((T, LANES))
    low = lane < HEAD_DIM
    pos = _f32(lax.broadcasted_iota(jnp.int32, (T, LANES), 0))
    for c in range(T // LANES):
        rel = pos - _f32(lane) - float(c * LANES)
        for h in range(N_HEADS):
            decay_ref[h, :, c * LANES:(c + 1) * LANES] = jnp.where(
                rel >= 0, jnp.exp(jnp.maximum(rel, 0.0) * _RET_LOG_G[h]), 0.0)
    for g in range(N_HEADS // 2):
        lg_lane = jnp.where(low, _RET_LOG_G[2 * g], _RET_LOG_G[2 * g + 1])
        xi_ref[g] = jnp.exp((pos + 1.0) * lg_lane)
        zeta_ref[g] = jnp.exp((T - 1.0 - pos) * lg_lane)


def _retention_scores(g, q2, kf, v2, state_ref, decay_ref, xi_ref, zeta_ref):
    T = q2.shape[0]
    low = _lane((T, LANES)) < HEAD_DIM
    row_low = lax.broadcasted_iota(jnp.int32, (LANES, LANES), 0) < HEAD_DIM
    same_head = row_low == (_lane((LANES, LANES)) < HEAD_DIM)
    k2 = kf.astype(MXU_DTYPE)
    R = state_ref[g]
    cross = _mm(q2, R.astype(MXU_DTYPE)) * xi_ref[g]
    kv = _mm_tn((kf * zeta_ref[g]).astype(MXU_DTYPE), v2)
    g_rows = jnp.where(row_low, math.exp(T * _RET_LOG_G[2 * g]), math.exp(T * _RET_LOG_G[2 * g + 1]))
    state_ref[g] = g_rows * R + jnp.where(same_head, kv, 0.0)
    scores = []
    for e in range(2):
        qm = jnp.where(low if e == 0 else ~low, q2, jnp.zeros_like(q2))
        scores.append((_mm_nt(qm, k2) * decay_ref[2 * g + e]).astype(MXU_DTYPE))
    return scores, cross


def _retention_output(scores, cross, v2, sg, gain):
    low = _lane(cross.shape) < HEAD_DIM
    out = jnp.where(low, _mm(scores[0], v2), _mm(scores[1], v2)) + cross
    mu = jnp.where(low,
                   jnp.sum(jnp.where(low, out, 0.0), axis=-1, keepdims=True),
                   jnp.sum(jnp.where(low, 0.0, out), axis=-1, keepdims=True)) * (1.0 / HEAD_DIM)
    dlt = out - mu
    sq = dlt * dlt
    var = jnp.where(low,
                    jnp.sum(jnp.where(low, sq, 0.0), axis=-1, keepdims=True),
                    jnp.sum(jnp.where(low, 0.0, sq), axis=-1, keepdims=True)) * (1.0 / HEAD_DIM)
    return (sg * (dlt * lax.rsqrt(var + GN_EPS) * gain)).astype(MXU_DTYPE)


def _merge_kernel(x_ref, gpre_ref, oa_ref, ob_ref, od_ref, rq_ref, rk_ref, rv_ref, sg_ref, gain_ref,
                  wg_ref, wb_ref, wo_ref, gpost_ref, y_ref, state_ref, decay_ref, xi_ref, zeta_ref,
                  *, tiles_per_seq):
    T, D = x_ref.shape

    @pl.when(pl.program_id(0) == 0)
    def _():
        _retention_tables(T, decay_ref, xi_ref, zeta_ref)

    @pl.when(pl.program_id(0) % tiles_per_seq == 0)
    def _():
        state_ref[...] = jnp.zeros_like(state_ref)

    half = T // 2
    rows = [slice(0, half), slice(half, 2 * half)]
    xs = [x_ref[r, :] for r in rows]
    hs = [_rms(x, gpre_ref[...]).astype(MXU_DTYPE) for x in xs]
    twice_merged = [None, None]

    def add_branch(n, branch):
        for k, (r, h) in enumerate(zip(rows, hs)):
            t = jnp.tanh(_mm(h, wg_ref[:, n * D:(n + 1) * D]))
            proj = _mm(branch[r, :], wb_ref[n])
            term = proj * t + proj
            twice_merged[k] = term if twice_merged[k] is None else twice_merged[k] + term

    pair_cols = [slice(g * LANES, (g + 1) * LANES) for g in range(N_HEADS // 2)]
    vals = [rv_ref[:, c] for c in pair_cols]
    ret = [_retention_scores(g, rq_ref[:, c], rk_ref[:, c], vals[g], state_ref, decay_ref, xi_ref, zeta_ref)
           for g, c in enumerate(pair_cols)]
    add_branch(0, oa_ref)
    add_branch(1, ob_ref)
    o_c = jnp.concatenate(
        [_retention_output(ret[g][0], ret[g][1], vals[g], sg_ref[:, c], gain_ref[:, c])
         for g, c in enumerate(pair_cols)], axis=1)
    add_branch(3, od_ref)
    add_branch(2, o_c)
    for r, x, tm2 in zip(rows, xs, twice_merged):
        mix = _mm(tm2.astype(MXU_DTYPE), wo_ref[...])
        y_ref[r, :] = x + _rms(mix, gpost_ref[...])


def _merge(x2, l, tiles_per_seq, gpre, oa, ob, od, rq, rk, rv, sg, gain, wg_half, wb, wo_half, gpost):
    N, D = x2.shape
    tm = ROW_TILE
    f32 = jnp.float32
    row = lambda w: pl.BlockSpec((tm, w), lambda i: (i, 0))
    full = lambda a: pl.BlockSpec(a.shape, lambda i: (0,) * a.ndim, pipeline_mode=pl.Buffered(1))
    wg, wo = wg_half, wo_half
    return pl.pallas_call(
        functools.partial(_merge_kernel, tiles_per_seq=tiles_per_seq),
        grid=(N // tm,),
        in_specs=[row(D), full(gpre), row(PAIR_W), row(PAIR_W), row(PAIR_W),
                  row(PAIR_W), row(PAIR_W), row(PAIR_W), row(PAIR_W), full(gain),
                  _layer(wg, l, 1), _layer(wb, l, 1), _layer(wo, l, 1), full(gpost)],
        out_specs=row(D),
        out_shape=jax.ShapeDtypeStruct((N, D), jnp.float32),
        scratch_shapes=[pltpu.VMEM((N_HEADS // 2, LANES, LANES), f32), pltpu.VMEM((N_HEADS, tm, tm), f32),
                        pltpu.VMEM((N_HEADS // 2, tm, LANES), f32), pltpu.VMEM((N_HEADS // 2, tm, LANES), f32)],
        compiler_params=pltpu.CompilerParams(dimension_semantics=("arbitrary",), vmem_limit_bytes=VMEM_LIMIT),
        name="merge",
    )(x2, gpre, oa, ob, od, rq, rk, rv, sg, gain, wg, wb, wo, gpost)


def _ffn_kernel(x_ref, gpre_ref, wgate_ref, wup_ref, wdown_ref, gpost_ref, y_ref):
    half = x_ref.shape[0] // 2
    rows = [slice(0, half), slice(half, 2 * half)]
    xs = [x_ref[r, :] for r in rows]
    hs = [_rms(x, gpre_ref[...]).astype(MXU_DTYPE) for x in xs]
    acts = []
    for h in hs:
        gate = _mm(h, wgate_ref[...])
        acts.append((gate / (1.0 + jnp.exp(-gate)) * _mm(h, wup_ref[...])).astype(MXU_DTYPE))
    for r, x, act in zip(rows, xs, acts):
        y_ref[r, :] = x + _rms(_mm(act, wdown_ref[...]), gpost_ref[...])


def _ffn(x2, l, gpre, wgate, wup, wdown, gpost):
    N, D = x2.shape
    tm = ROW_TILE
    row = pl.BlockSpec((tm, D), lambda i: (i, 0))
    full = lambda a: pl.BlockSpec(a.shape, lambda i: (0,) * a.ndim, pipeline_mode=pl.Buffered(1))
    return pl.pallas_call(
        _ffn_kernel,
        grid=(N // tm,),
        in_specs=[row, full(gpre), _layer(wgate, l, 1), _layer(wup, l, 1), _layer(wdown, l, 1), full(gpost)],
        out_specs=row,
        out_shape=jax.ShapeDtypeStruct((N, D), jnp.float32),
        compiler_params=pltpu.CompilerParams(dimension_semantics=("arbitrary",), vmem_limit_bytes=VMEM_LIMIT),
        name="ffn",
    )(x2, gpre, wgate, wup, wdown, gpost)


def _rope_tables(S, half, lo):
    period = 2 * half if lo == 0 else LANES
    inv = ROPE_THETA ** (-jnp.arange(half, dtype=jnp.float32) / half)
    ang = jnp.arange(S, dtype=jnp.float32)[:, None] * inv[None, :]
    cos, sin = jnp.cos(ang), jnp.sin(ang)
    reps = LANES // period
    pad_lo = jnp.zeros((S, lo), jnp.float32)
    pad_hi = jnp.zeros((S, period - lo - 2 * half), jnp.float32)
    zeros = jnp.zeros((S, half), jnp.float32)
    cos_t = jnp.concatenate([pad_lo + 1.0, cos, cos, pad_hi + 1.0], axis=1)
    sin_up = jnp.concatenate([pad_lo, -sin, zeros, pad_hi], axis=1)
    sin_dn = jnp.concatenate([pad_lo, zeros, sin, pad_hi], axis=1)
    return tuple(jnp.tile(t, (1, reps)) for t in (cos_t, sin_up, sin_dn))


def _pad_heads(w, width=HEAD_DIM):
    K = w.shape[0]
    w = w.reshape(K, N_HEADS, width)
    return jnp.pad(w, ((0, 0), (0, 0), (0, LANES - width))).reshape(K, PAD_W)


def _layer_weights(b_forget, w_uq, w_ukv):
    bf = jnp.pad(b_forget, (0, LANES - N_HEADS)).reshape(1, LANES)
    wuq = _pad_heads(w_uq, MLA_NOPE + MLA_ROPE).astype(MXU_DTYPE)
    ukv = w_ukv.reshape(MLA_KV_RANK, N_HEADS, MLA_NOPE + MLA_V)
    wuk = _pad_heads(ukv[:, :, :MLA_NOPE].reshape(MLA_KV_RANK, -1)).astype(MXU_DTYPE)
    wuv = _pad_heads(ukv[:, :, MLA_NOPE:].reshape(MLA_KV_RANK, -1)).astype(MXU_DTYPE)
    return bf, wuq, wuk, wuv


def kernel(x, w_in, b_forget, ret_gn_gain, mla_q_norm, mla_kv_norm, w_uq, w_ukv, w_gate, w_branch, w_out,
           g_pre_mix, g_post_mix, g_pre_ffn, g_post_ffn, w_ffn_gate, w_ffn_up, w_ffn_down):
    B, S, D = x.shape
    depth = w_in.shape[0]
    assert S % ROW_TILE == 0
    ret_tabs = _rope_tables(S, HEAD_DIM // 2, 0)
    mla_tabs = _rope_tables(S, MLA_ROPE // 2, MLA_NOPE)
    r1 = lambda v: v.reshape(1, -1)
    assert w_in.shape[2] == IN_KR_COL + MLA_ROPE
    wg_half = (0.5 * w_gate).astype(MXU_DTYPE)
    wo_half = (0.5 * w_out).astype(MXU_DTYPE)
    wb = w_branch.astype(MXU_DTYPE)
    wf_gate, wf_up, wf_down = (w.astype(MXU_DTYPE) for w in (w_ffn_gate, w_ffn_up, w_ffn_down))
    for l in range(depth):
        bf, wuq, wuk, wuv = _layer_weights(b_forget[l], w_uq[l], w_ukv[l])
        (fq, fk, fva, dq, dk, dva, rq, rk, rv, sg, mq, mk, mva) = _inproj(
            x, r1(g_pre_mix[l]), w_in[l].T, bf, ret_tabs, mla_tabs,
            r1(mla_q_norm[l]), r1(mla_kv_norm[l]), wuq, wuk, wuv)
        o_a = _flash(fq, fk, fva, True)
        o_b = _dilated(dq, dk, dva)
        o_d = _flash(mq, mk, mva, False)
        flat = lambda a: a.reshape(B * S, a.shape[-1])
        x2 = _merge(flat(x), l, S // ROW_TILE, r1(g_pre_mix[l]), flat(o_a), flat(o_b), flat(o_d),
                    flat(rq), flat(rk), flat(rv), flat(sg), r1(ret_gn_gain[l]),
                    wg_half, wb, wo_half, r1(g_post_mix[l]))
        x2 = _ffn(x2, l, r1(g_pre_ffn[l]), wf_gate, wf_up, wf_down, r1(g_post_ffn[l]))
        x = x2.reshape(B, S, D)
    return x
```

```python
import functools
import math

import jax
import jax.numpy as jnp
from jax import lax
from jax.experimental import pallas as pl
from jax.experimental.pallas import tpu as pltpu

HEAD_DIM = 64
N_HEADS = 4
PAIR_W = N_HEADS * HEAD_DIM
LANES = 128
PAD_W = N_HEADS * LANES
DIL_PATTERNS = ((128, 1), (512, 4), (2048, 16))
DIL_BLOCK = 128
DIL_UNROLL = 8
FLASH_Q_BLOCKS = 4
MLA_Q_RANK = 256
MLA_KV_RANK = 128
MLA_NOPE = 64
MLA_ROPE = 32
MLA_V = 64
ROPE_THETA = 10000.0
RMS_EPS = 1e-6
GN_EPS = 1e-5
NEG_BIG = -1e30
LOG2E = math.log2(math.e)

MXU_DTYPE = jnp.bfloat16
ROW_TILE = 512
VMEM_LIMIT = 56 * 1024 * 1024

_C_FQ, _C_FK, _C_FV = 0, 256, 512
_C_DQ, _C_DK, _C_DV = 768, 1024, 1280
_C_RQ, _C_RK, _C_RV, _C_RG = 1536, 1792, 2048, 2304
_C_CQ, _C_TAIL = 2560, 2816
IN_CAT_W = 3072
IN_FF_COL = 3 * PAIR_W
IN_KR_COL = 2948


def _f32(x):
    return x.astype(jnp.float32)


def _rms(x, g):
    ms = jnp.mean(x * x, axis=-1, keepdims=True)
    return x * lax.rsqrt(ms + RMS_EPS) * g


def _mm(a, b):
    return jnp.dot(a, b, preferred_element_type=jnp.float32)


def _mm_nt(a, b):
    return lax.dot_general(a, b, (((1,), (1,)), ((), ())), preferred_element_type=jnp.float32)


def _mm_tn(a, b):
    return lax.dot_general(a, b, (((0,), (0,)), ((), ())), preferred_element_type=jnp.float32)


def _lane(shape):
    return lax.broadcasted_iota(jnp.int32, shape, len(shape) - 1)


def _rope_chunks(a, cos, sin_a, sin_b, half):
    outs = []
    for c in range(a.shape[1] // LANES):
        t = a[:, c * LANES:(c + 1) * LANES]
        up = pltpu.roll(t, LANES - half, axis=1)
        dn = pltpu.roll(t, half, axis=1)
        outs.append(t * cos + up * sin_a + dn * sin_b)
    return outs[0] if len(outs) == 1 else jnp.concatenate(outs, axis=1)


def _inproj_kernel(x_ref, g_ref, w_ref, bf_ref, rcos_ref, rsa_ref, rsb_ref, mcos_ref, msa_ref, msb_ref,
                   qn_ref, kvn_ref, wuq_ref, wuk_ref, wuv_ref,
                   fq_ref, fk_ref, fva_ref,
                   dq_ref, dk_ref, dva_ref,
                   rq_ref, rk_ref, rv_ref, sg_ref,
                   mq_ref, mk_ref, mva_ref,
                   carry_ref, wb_ref, *, mla_scale):
    tm = x_ref.shape[0]

    @pl.when((pl.program_id(0) == 0) & (pl.program_id(1) == 0))
    def _():
        n_ff = N_HEADS
        for r in range(0, IN_FF_COL, PAIR_W):
            wb_ref[r:r + PAIR_W, :] = w_ref[r:r + PAIR_W, :].astype(wb_ref.dtype)
        for r in range(IN_FF_COL, _C_TAIL + LANES, LANES):
            wb_ref[r:r + LANES, :] = w_ref[r + n_ff:r + n_ff + LANES, :].astype(wb_ref.dtype)
        blk = _C_TAIL + LANES
        wb_ref[blk:blk + LANES, :] = jnp.zeros((LANES, wb_ref.shape[1]), wb_ref.dtype)
        head = w_ref[IN_FF_COL:IN_FF_COL + 2 * n_ff, :]
        is_ff = lax.broadcasted_iota(jnp.int32, head.shape, 0) < n_ff
        wb_ref[blk:blk + 2 * n_ff, :] = jnp.where(is_ff, head, 0.0).astype(wb_ref.dtype)
        wb_ref[blk + MLA_NOPE:blk + MLA_NOPE + MLA_ROPE, :] = (
            w_ref[IN_KR_COL:IN_KR_COL + MLA_ROPE, :].astype(wb_ref.dtype))

    @pl.when(pl.program_id(1) == 0)
    def _():
        carry_ref[...] = jnp.zeros_like(carry_ref)

    h = _rms(x_ref[...], g_ref[...]).astype(MXU_DTYPE)

    def proj(lo, width):
        return _mm_nt(h, wb_ref[lo:lo + width, :])

    ones_up = jnp.where(_lane((1, PAD_W)) % LANES >= HEAD_DIM, 1.0, 0.0)
    mcos, msa, msb = mcos_ref[...], msa_ref[...], msb_ref[...]
    rcos, rsa, rsb = rcos_ref[...], rsa_ref[...], rsb_ref[...]

    def with_ones(v):
        out = []
        for hd in range(N_HEADS):
            chunk = v[:, (hd // 2) * LANES:(hd // 2 + 1) * LANES]
            if hd % 2:
                chunk = pltpu.roll(chunk, HEAD_DIM, axis=1)
            out.append(jnp.where(_lane(chunk.shape) < HEAD_DIM, chunk, 1.0))
        return jnp.concatenate(out, axis=1)


    tail = proj(_C_TAIL, 2 * LANES)
    cq_raw = proj(_C_CQ, MLA_Q_RANK)
    ffkr = tail[:, LANES:2 * LANES]
    ff = ffkr + bf_ref[...]
    log_f = jnp.minimum(ff, 0.0) - jnp.log1p(jnp.exp(-jnp.abs(ff)))
    rows = lax.broadcasted_iota(jnp.int32, (tm, LANES), 0)
    cs = log_f
    step = 1
    while step < tm:
        cs = cs + jnp.where(rows >= step, pltpu.roll(cs, step, axis=0), 0.0)
        step *= 2
    cs = cs + carry_ref[...]
    carry_ref[...] = cs[tm - 1:tm, :]
    cs2 = cs * LOG2E
    hi = _f32(cs2.astype(MXU_DTYPE))
    rem = cs2 - hi
    mid = _f32(rem.astype(MXU_DTYPE))
    lo = rem - mid
    lane = _lane(cs2.shape)
    split = jnp.where(lane < N_HEADS, hi,
                      jnp.where(lane < 2 * N_HEADS, pltpu.roll(mid, N_HEADS, axis=1),
                                jnp.where(lane < 3 * N_HEADS, pltpu.roll(lo, 2 * N_HEADS, axis=1), 0.0)))
    fk_ref[:, PAIR_W:PAIR_W + LANES] = (-split).astype(fk_ref.dtype)

    rq_ref[...] = _rope_chunks(proj(_C_RQ, PAIR_W), rcos, rsa, rsb, HEAD_DIM // 2).astype(rq_ref.dtype)
    rk_ref[...] = _rope_chunks(proj(_C_RK, PAIR_W), rcos, rsa, rsb, HEAD_DIM // 2) * HEAD_DIM ** -0.5
    rg = proj(_C_RG, PAIR_W)
    sg_ref[...] = rg / (1.0 + jnp.exp(-rg))

    ckv = _rms(tail[:, 0:LANES], kvn_ref[...]).astype(MXU_DTYPE)
    rope_lanes = (_lane(ffkr.shape) >= MLA_NOPE) & (_lane(ffkr.shape) < MLA_NOPE + MLA_ROPE)
    kr = jnp.where(rope_lanes, _rope_chunks(ffkr, mcos, msa, msb, MLA_ROPE // 2), 0.0)
    mk_ref[...] = (_mm(ckv, wuk_ref[...]) + jnp.concatenate([kr] * N_HEADS, axis=1)).astype(mk_ref.dtype)
    mva_ref[...] = (_mm(ckv, wuv_ref[...]) + ones_up).astype(mva_ref.dtype)
    cq = _rms(cq_raw, qn_ref[...]).astype(MXU_DTYPE)
    q = _rope_chunks(_mm(cq, wuq_ref[...]), mcos, msa, msb, MLA_ROPE // 2)
    mq_ref[...] = (q * mla_scale).astype(mq_ref.dtype)

    fva_ref[...] = with_ones(proj(_C_FV, PAIR_W)).astype(fva_ref.dtype)
    dva_ref[...] = with_ones(proj(_C_DV, PAIR_W)).astype(dva_ref.dtype)
    fq_ref[...] = (proj(_C_FQ, PAIR_W) * (HEAD_DIM ** -0.5 * LOG2E)).astype(fq_ref.dtype)
    dq_ref[...] = (proj(_C_DQ, PAIR_W) * (HEAD_DIM ** -0.5 * LOG2E)).astype(dq_ref.dtype)

    fk_ref[:, 0:PAIR_W] = proj(_C_FK, PAIR_W).astype(fk_ref.dtype)
    dk_ref[...] = proj(_C_DK, PAIR_W).astype(dk_ref.dtype)
    rv_ref[...] = proj(_C_RV, PAIR_W).astype(rv_ref.dtype)


def _layer(a, l, n_grid):
    zeros = (0,) * (a.ndim - 1)
    index = (lambda b, i: (l,) + zeros) if n_grid == 2 else (lambda i: (l,) + zeros)
    return pl.BlockSpec((None,) + a.shape[1:], index, pipeline_mode=pl.Buffered(1))


def _inproj(x, g, w_in, bf, ret_tabs, mla_tabs, qn, kvn, wuq, wuk, wuv):
    B, S, D = x.shape
    tm = ROW_TILE
    ns = S // tm
    row = lambda w: pl.BlockSpec((None, tm, w), lambda b, i: (b, i, 0))
    full = lambda a: pl.BlockSpec(a.shape, lambda b, i: (0,) * a.ndim, pipeline_mode=pl.Buffered(1))
    tab = pl.BlockSpec((tm, LANES), lambda b, i: (i, 0))
    bf16, f32 = MXU_DTYPE, jnp.float32
    out_defs = [
        (PAIR_W, bf16), (PAIR_W + LANES, bf16), (PAD_W, bf16),
        (PAIR_W, bf16), (PAIR_W, bf16), (PAD_W, bf16),
        (PAIR_W, bf16), (PAIR_W, f32), (PAIR_W, bf16), (PAIR_W, f32),
        (PAD_W, bf16), (PAD_W, bf16), (PAD_W, bf16),
    ]
    out_shape, out_specs = [], []
    for d in out_defs:
        out_shape.append(jax.ShapeDtypeStruct((B, S, d[0]), d[1]))
        out_specs.append(row(d[0]))
    tabs = list(ret_tabs) + list(mla_tabs)
    tail = [qn, kvn, wuq, wuk, wuv]
    return pl.pallas_call(
        functools.partial(_inproj_kernel, mla_scale=(MLA_NOPE + MLA_ROPE) ** -0.5 * LOG2E),
        grid=(B, ns),
        in_specs=[row(D), full(g), full(w_in), full(bf)] + [tab] * 6 + [full(a) for a in tail],
        out_specs=out_specs,
        out_shape=out_shape,
        scratch_shapes=[pltpu.VMEM((1, LANES), f32), pltpu.VMEM((IN_CAT_W, D), bf16)],
        compiler_params=pltpu.CompilerParams(
            dimension_semantics=("arbitrary", "arbitrary"), vmem_limit_bytes=VMEM_LIMIT),
        name="inproj",
    )(x, g, w_in, bf, *tabs, *tail)


def _head_q(q_ref, h, pair_q):
    if pair_q:
        g, e = divmod(h, 2)
        q2 = q_ref[:, g * LANES:(g + 1) * LANES]
        keep = (_lane(q2.shape) >= HEAD_DIM) if e else (_lane(q2.shape) < HEAD_DIM)
        return jnp.where(keep, q2, jnp.zeros_like(q2)), g
    return q_ref[:, h * LANES:(h + 1) * LANES], h


def _normalise_pair(acc_even, acc_odd):
    even = acc_even / pltpu.roll(acc_even, HEAD_DIM, axis=1)
    odd = pltpu.roll(acc_odd, HEAD_DIM, axis=1) / acc_odd
    return jnp.where(_lane(even.shape) < HEAD_DIM, even, odd)


def _flash_kernel(q_ref, k_ref, va_ref, o_ref, qop_ref, m_ref, acc_ref, s0_ref, *, fox, tk):
    tq = q_ref.shape[0]
    n_blk = tq // tk
    qi = pl.program_id(1)
    for h in range(N_HEADS):
        if fox:
            qm, _ = _head_q(q_ref, h, True)
            lane = _lane(qm.shape)
            ones = jnp.where((lane % N_HEADS == h) & (lane < 3 * N_HEADS), 1.0, 0.0).astype(qm.dtype)
            qop_ref[h] = jnp.concatenate([qm, ones], axis=1)
        else:
            qop_ref[h] = q_ref[:, h * LANES:(h + 1) * LANES]
    m_ref[...] = jnp.full(m_ref.shape, NEG_BIG, jnp.float32)
    acc_ref[...] = jnp.zeros(acc_ref.shape, jnp.float32)

    def keys(j, h):
        start = pl.multiple_of(j * tk, tk)
        if fox:
            g = h // 2
            return jnp.concatenate([k_ref[pl.ds(start, tk), g * LANES:(g + 1) * LANES],
                                    k_ref[pl.ds(start, tk), PAIR_W:PAIR_W + LANES]], axis=1)
        return k_ref[pl.ds(start, tk), h * LANES:(h + 1) * LANES]

    def scores(h, j, rows):
        return _mm_nt(qop_ref[h, rows, :], keys(j, h))

    def step(j, rows, causal, next_rows):
        n_rows = rows.stop - rows.start
        start = pl.multiple_of(j * tk, tk)
        s_next = s0_ref[0:n_rows, :]
        for h in range(N_HEADS):
            s = s_next
            if h + 1 < N_HEADS:
                s_next = scores(h + 1, j, rows)
            elif next_rows is not None:
                s0_ref[0:next_rows.stop - next_rows.start, :] = scores(0, j + 1, next_rows)
            if causal:
                r = lax.broadcasted_iota(jnp.int32, s.shape, 0)
                c = lax.broadcasted_iota(jnp.int32, s.shape, 1)
                s = jnp.where(c <= r, s, NEG_BIG)
            m = m_ref[h, rows, :]
            m_new = jnp.maximum(m, jnp.max(s, axis=-1, keepdims=True))
            p = jnp.concatenate([jnp.exp2(s[:, c * LANES:(c + 1) * LANES] - m_new) for c in range(tk // LANES)],
                                axis=1).astype(MXU_DTYPE)
            acc_ref[h, rows, :] = (jnp.exp2(m - m_new) * acc_ref[h, rows, :]
                                   + _mm(p, va_ref[pl.ds(start, tk), h * LANES:(h + 1) * LANES]))
            m_ref[h, rows, :] = m_new

    every = slice(0, tq)

    def body(j, carry):
        step(j, every, False, every)
        return carry

    s0_ref[...] = scores(0, 0, every)
    lax.fori_loop(0, n_blk * qi, body, 0)
    for d in range(n_blk):
        step(n_blk * qi + d, slice(d * tk, tq), True, slice((d + 1) * tk, tq) if d + 1 < n_blk else None)
    for g in range(N_HEADS // 2):
        o_ref[:, g * LANES:(g + 1) * LANES] = _normalise_pair(acc_ref[2 * g], acc_ref[2 * g + 1]).astype(o_ref.dtype)


def _flash(q, k, va, fox):
    B, S, qw = q.shape
    tk = ROW_TILE
    tq = FLASH_Q_BLOCKS * tk
    assert S % tq == 0
    seq = lambda w: pl.BlockSpec((None, S, w), lambda b, i: (b, 0, 0))
    return pl.pallas_call(
        functools.partial(_flash_kernel, fox=fox, tk=tk),
        grid=(B, S // tq),
        in_specs=[pl.BlockSpec((None, tq, qw), lambda b, i: (b, i, 0)), seq(k.shape[-1]), seq(PAD_W)],
        out_specs=pl.BlockSpec((None, tq, PAIR_W), lambda b, i: (b, i, 0)),
        out_shape=jax.ShapeDtypeStruct((B, S, PAIR_W), MXU_DTYPE),
        scratch_shapes=[pltpu.VMEM((N_HEADS, tq, 2 * LANES if fox else LANES), MXU_DTYPE),
                        pltpu.VMEM((N_HEADS, tq, LANES), jnp.float32),
                        pltpu.VMEM((N_HEADS, tq, LANES), jnp.float32),
                        pltpu.VMEM((tq, tk), jnp.float32)],
        compiler_params=pltpu.CompilerParams(
            dimension_semantics=("arbitrary", "arbitrary"), vmem_limit_bytes=VMEM_LIMIT),
        name="fox_attn" if fox else "mla_attn",
    )(q, k, va)


def _dilated_kernel(q_ref, k_ref, va_ref, o_ref, qf_ref, kf_ref, vf_ref, acc_ref, stage_ref):
    S = q_ref.shape[0]
    n = DIL_BLOCK
    tile = ROW_TILE
    quarter = S // 4
    run = n // 4
    keep_stats = _lane((n, LANES)) < HEAD_DIM + HEAD_DIM // 2

    def widen(i, carry):
        rows = pl.ds(pl.multiple_of(i * tile, tile), tile)
        slot = 0
        for src, dst in ((q_ref, qf_ref), (k_ref, kf_ref), (va_ref, vf_ref)):
            for c in range(dst.shape[0]):
                stage_ref[slot] = _f32(src[rows, c * LANES:(c + 1) * LANES])
                for cls in range(4):
                    dst[c, pl.ds(cls * quarter + i * (tile // 4), tile // 4), :] = (
                        stage_ref[slot, pl.ds(cls, tile // 4, stride=4), :])
                slot += 1
        return carry

    lax.fori_loop(0, S // tile, widen, 0)

    ri = lax.broadcasted_iota(jnp.int32, (n, 2 * n), 0)
    ci = lax.broadcasted_iota(jnp.int32, (n, 2 * n), 1)
    diff_class = ri - ci
    diff_runs = (4 * (jnp.bitwise_and(ri, run - 1) - jnp.bitwise_and(ci, 2 * run - 1))
                 + jnp.right_shift(ri, run.bit_length() - 1) - jnp.right_shift(ci, (2 * run).bit_length() - 1))

    def plan(dil, u):
        if dil == 1:
            first = u == 0
            q_runs = [(cls * quarter + run * u, run, 1) for cls in range(4)]
            w_runs = [(cls * quarter + run * jnp.maximum(u - 1, 0), 2 * run, 1) for cls in range(4)]
            return q_runs, w_runs, diff_runs, first
        if dil == 4:
            cls, a = jnp.bitwise_and(u, 3), jnp.right_shift(u, 2)
            base, step, stride = cls * quarter, n, 1
        else:
            rho, a = jnp.bitwise_and(u, 15), jnp.right_shift(u, 4)
            base, step, stride = jnp.bitwise_and(rho, 3) * quarter + jnp.right_shift(rho, 2), 4 * n, 4
        q_runs = [(base + step * a, n, stride)]
        w_runs = [(base + step * jnp.maximum(a - 1, 0), 2 * n, stride)]
        return q_runs, w_runs, diff_class, a == 0

    def rows_of(r):
        start, size, stride = r
        return pl.ds(start, size, stride=stride) if stride > 1 else pl.ds(start, size)

    def load(ref, idx, runs):
        parts = [ref[idx, rows_of(r), :] for r in runs]
        return parts[0] if len(parts) == 1 else jnp.concatenate(parts, axis=0)

    def store(ref, idx, runs, val):
        at = 0
        for r in runs:
            ref[idx, rows_of(r), :] = val[at:at + r[1], :]
            at += r[1]

    for number, (_, dil) in enumerate(DIL_PATTERNS):

        unroll = DIL_UNROLL if number > 0 else 2 * DIL_UNROLL

        def blocks(t, carry, dil=dil, merge=number > 0, unroll=unroll):
            todo = []
            for sub in range(unroll):
                q_runs, w_runs, diff, first = plan(dil, t * unroll + sub)
                dist = diff + jnp.where(first, 0, n)
                valid = (dist >= 0) & (dist <= n)
                q2 = [load(qf_ref, g, q_runs).astype(MXU_DTYPE) for g in range(N_HEADS // 2)]
                k2 = [load(kf_ref, g, w_runs).astype(MXU_DTYPE) for g in range(N_HEADS // 2)]
                v4 = [load(vf_ref, h, w_runs).astype(MXU_DTYPE) for h in range(N_HEADS)]
                old = [load(acc_ref, h, q_runs) for h in range(N_HEADS)] if merge else None
                m_old = [jnp.broadcast_to(o[:, LANES - 1:LANES], (n, LANES)) for o in old] if merge else None
                todo.append((q_runs, valid, q2, k2, v4, old, m_old))
            done = []
            for q_runs, valid, q2, k2, v4, old, m_old in todo:
                for h in range(N_HEADS):
                    g, e = divmod(h, 2)
                    keep = (_lane(q2[g].shape) >= HEAD_DIM) if e else (_lane(q2[g].shape) < HEAD_DIM)
                    qh = jnp.where(keep, q2[g], jnp.zeros_like(q2[g]))
                    s = jnp.where(valid, _mm_nt(qh, k2[g]), NEG_BIG)
                    m_new = jnp.max(s, axis=-1, keepdims=True)
                    if merge:
                        m_new = jnp.maximum(m_old[h], m_new)
                    p = jnp.concatenate([jnp.exp2(s[:, c * LANES:(c + 1) * LANES] - m_new) for c in range(2)],
                                        axis=1).astype(MXU_DTYPE)
                    upd = _mm(p, v4[h])
                    if merge:
                        upd = jnp.exp2(m_old[h] - m_new) * old[h] + upd
                    done.append((h, q_runs, jnp.where(keep_stats, upd, m_new)))
            for h, q_runs, val in done:
                store(acc_ref, h, q_runs, val)
            return carry

        lax.fori_loop(0, S // n // unroll, blocks, 0)

    def finish(i, carry):
        rows = pl.ds(pl.multiple_of(i * tile, tile), tile)
        outs = []
        for h in range(N_HEADS):
            for cls in range(4):
                a_h = acc_ref[h, pl.ds(cls * quarter + i * (tile // 4), tile // 4), :]
                stage_ref[h, pl.ds(cls, tile // 4, stride=4), :] = a_h / a_h[:, HEAD_DIM:HEAD_DIM + 1]
            outs.append(stage_ref[h])
        for g in range(N_HEADS // 2):
            pair = jnp.where(_lane(outs[2 * g].shape) < HEAD_DIM, outs[2 * g],
                             pltpu.roll(outs[2 * g + 1], HEAD_DIM, axis=1))
            o_ref[rows, g * LANES:(g + 1) * LANES] = pair.astype(o_ref.dtype)
        return carry

    lax.fori_loop(0, S // tile, finish, 0)


def _dilated(q, k, va):
    B, S, _ = q.shape
    assert S % (DIL_PATTERNS[-1][1] * 2 * DIL_BLOCK) == 0 and S % ROW_TILE == 0
    seq = lambda w: pl.BlockSpec((None, S, w), lambda b: (b, 0, 0))
    f32 = jnp.float32
    n_chunks = (2 * PAIR_W + PAD_W) // LANES
    return pl.pallas_call(
        _dilated_kernel,
        grid=(B,),
        in_specs=[seq(PAIR_W), seq(PAIR_W), seq(PAD_W)],
        out_specs=seq(PAIR_W),
        out_shape=jax.ShapeDtypeStruct((B, S, PAIR_W), MXU_DTYPE),
        scratch_shapes=[pltpu.VMEM((PAIR_W // LANES, S, LANES), f32), pltpu.VMEM((PAIR_W // LANES, S, LANES), f32),
                        pltpu.VMEM((N_HEADS, S, LANES), f32), pltpu.VMEM((N_HEADS, S, LANES), f32),
                        pltpu.VMEM((n_chunks, ROW_TILE, LANES), f32)],
        compiler_params=pltpu.CompilerParams(dimension_semantics=("arbitrary",), vmem_limit_bytes=VMEM_LIMIT),
        name="dilated",
    )(q, k, va)


_RET_LOG_G = [math.log1p(-(2.0 ** (-5.0 - h))) for h in range(N_HEADS)]


def _retention_tables(T, decay_ref, xi_ref, zeta_ref):
    lane = _lane((T, LANES))
    low = lane < HEAD_DIM
    pos = _f32(lax.broadcasted_iota(jnp.int32, (T, LANES), 0))
    for c in range(T // LANES):
        rel = pos - _f32(lane) - float(c * LANES)
        for h in range(N_HEADS):
            decay_ref[h, :, c * LANES:(c + 1) * LANES] = jnp.where(
                rel >= 0, jnp.exp(jnp.maximum(rel, 0.0) * _RET_LOG_G[h]), 0.0)
    for g in range(N_HEADS // 2):
        lg_lane = jnp.where(low, _RET_LOG_G[2 * g], _RET_LOG_G[2 * g + 1])
        xi_ref[g] = jnp.exp((pos + 1.0) * lg_lane)
        zeta_ref[g] = jnp.exp((T - 1.0 - pos) * lg_lane)


def _retention_scores(g, q2, kf, v2, state_ref, decay_ref, xi_ref, zeta_ref):
    T = q2.shape[0]
    low = _lane((T, LANES)) < HEAD_DIM
    row_low = lax.broadcasted_iota(jnp.int32, (LANES, LANES), 0) < HEAD_DIM
    same_head = row_low == (_lane((LANES, LANES)) < HEAD_DIM)
    k2 = kf.astype(MXU_DTYPE)
    R = state_ref[g]
    cross = _mm(q2, R.astype(MXU_DTYPE)) * xi_ref[g]
    kv = _mm_tn((kf * zeta_ref[g]).astype(MXU_DTYPE), v2)
    g_rows = jnp.where(row_low, math.exp(T * _RET_LOG_G[2 * g]), math.exp(T * _RET_LOG_G[2 * g + 1]))
    state_ref[g] = g_rows * R + jnp.where(same_head, kv, 0.0)
    scores = []
    for e in range(2):
        qm = jnp.where(low if e == 0 else ~low, q2, jnp.zeros_like(q2))
        scores.append((_mm_nt(qm, k2) * decay_ref[2 * g + e]).astype(MXU_DTYPE))
    return scores, cross


def _retention_output(scores, cross, v2, sg, gain):
    low = _lane(cross.shape) < HEAD_DIM
    out = jnp.where(low, _mm(scores[0], v2), _mm(scores[1], v2)) + cross
    mu = jnp.where(low,
                   jnp.sum(jnp.where(low, out, 0.0), axis=-1, keepdims=True),
                   jnp.sum(jnp.where(low, 0.0, out), axis=-1, keepdims=True)) * (1.0 / HEAD_DIM)
    dlt = out - mu
    sq = dlt * dlt
    var = jnp.where(low,
                    jnp.sum(jnp.where(low, sq, 0.0), axis=-1, keepdims=True),
                    jnp.sum(jnp.where(low, 0.0, sq), axis=-1, keepdims=True)) * (1.0 / HEAD_DIM)
    return (sg * (dlt * lax.rsqrt(var + GN_EPS) * gain)).astype(MXU_DTYPE)


def _merge_kernel(x_ref, gpre_ref, oa_ref, ob_ref, od_ref, rq_ref, rk_ref, rv_ref, sg_ref, gain_ref,
                  wg_ref, wb_ref, wo_ref, gpost_ref, y_ref, state_ref, decay_ref, xi_ref, zeta_ref,
                  *, tiles_per_seq):
    T, D = x_ref.shape

    @pl.when(pl.program_id(0) == 0)
    def _():
        _retention_tables(T, decay_ref, xi_ref, zeta_ref)

    @pl.when(pl.program_id(0) % tiles_per_seq == 0)
    def _():
        state_ref[...] = jnp.zeros_like(state_ref)

    half = T // 2
    rows = [slice(0, half), slice(half, 2 * half)]
    xs = [x_ref[r, :] for r in rows]
    hs = [_rms(x, gpre_ref[...]).astype(MXU_DTYPE) for x in xs]
    twice_merged = [None, None]

    def add_branch(n, branch):
        for k, (r, h) in enumerate(zip(rows, hs)):
            t = jnp.tanh(_mm(h, wg_ref[:, n * D:(n + 1) * D]))
            proj = _mm(branch[r, :], wb_ref[n])
            term = proj * t + proj
            twice_merged[k] = term if twice_merged[k] is None else twice_merged[k] + term

    pair_cols = [slice(g * LANES, (g + 1) * LANES) for g in range(N_HEADS // 2)]
    vals = [rv_ref[:, c] for c in pair_cols]
    ret = [_retention_scores(g, rq_ref[:, c], rk_ref[:, c], vals[g], state_ref, decay_ref, xi_ref, zeta_ref)
           for g, c in enumerate(pair_cols)]
    add_branch(0, oa_ref)
    add_branch(1, ob_ref)
    o_c = jnp.concatenate(
        [_retention_output(ret[g][0], ret[g][1], vals[g], sg_ref[:, c], gain_ref[:, c])
         for g, c in enumerate(pair_cols)], axis=1)
    add_branch(3, od_ref)
    add_branch(2, o_c)
    for r, x, tm2 in zip(rows, xs, twice_merged):
        mix = _mm(tm2.astype(MXU_DTYPE), wo_ref[...])
        y_ref[r, :] = x + _rms(mix, gpost_ref[...])


def _merge(x2, l, tiles_per_seq, gpre, oa, ob, od, rq, rk, rv, sg, gain, wg_half, wb, wo_half, gpost):
    N, D = x2.shape
    tm = ROW_TILE
    f32 = jnp.float32
    row = lambda w: pl.BlockSpec((tm, w), lambda i: (i, 0))
    full = lambda a: pl.BlockSpec(a.shape, lambda i: (0,) * a.ndim, pipeline_mode=pl.Buffered(1))
    wg, wo = wg_half, wo_half
    return pl.pallas_call(
        functools.partial(_merge_kernel, tiles_per_seq=tiles_per_seq),
        grid=(N // tm,),
        in_specs=[row(D), full(gpre), row(PAIR_W), row(PAIR_W), row(PAIR_W),
                  row(PAIR_W), row(PAIR_W), row(PAIR_W), row(PAIR_W), full(gain),
                  _layer(wg, l, 1), _layer(wb, l, 1), _layer(wo, l, 1), full(gpost)],
        out_specs=row(D),
        out_shape=jax.ShapeDtypeStruct((N, D), jnp.float32),
        scratch_shapes=[pltpu.VMEM((N_HEADS // 2, LANES, LANES), f32), pltpu.VMEM((N_HEADS, tm, tm), f32),
                        pltpu.VMEM((N_HEADS // 2, tm, LANES), f32), pltpu.VMEM((N_HEADS // 2, tm, LANES), f32)],
        compiler_params=pltpu.CompilerParams(dimension_semantics=("arbitrary",), vmem_limit_bytes=VMEM_LIMIT),
        name="merge",
    )(x2, gpre, oa, ob, od, rq, rk, rv, sg, gain, wg, wb, wo, gpost)


def _ffn_kernel(x_ref, gpre_ref, wgate_ref, wup_ref, wdown_ref, gpost_ref, y_ref):
    half = x_ref.shape[0] // 2
    rows = [slice(0, half), slice(half, 2 * half)]
    xs = [x_ref[r, :] for r in rows]
    hs = [_rms(x, gpre_ref[...]).astype(MXU_DTYPE) for x in xs]
    acts = []
    for h in hs:
        gate = _mm(h, wgate_ref[...])
        acts.append((gate / (1.0 + jnp.exp(-gate)) * _mm(h, wup_ref[...])).astype(MXU_DTYPE))
    for r, x, act in zip(rows, xs, acts):
        y_ref[r, :] = x + _rms(_mm(act, wdown_ref[...]), gpost_ref[...])


def _ffn(x2, l, gpre, wgate, wup, wdown, gpost):
    N, D = x2.shape
    tm = ROW_TILE
    row = pl.BlockSpec((tm, D), lambda i: (i, 0))
    full = lambda a: pl.BlockSpec(a.shape, lambda i: (0,) * a.ndim, pipeline_mode=pl.Buffered(1))
    return pl.pallas_call(
        _ffn_kernel,
        grid=(N // tm,),
        in_specs=[row, full(gpre), _layer(wgate, l, 1), _layer(wup, l, 1), _layer(wdown, l, 1), full(gpost)],
        out_specs=row,
        out_shape=jax.ShapeDtypeStruct((N, D), jnp.float32),
        compiler_params=pltpu.CompilerParams(dimension_semantics=("arbitrary",), vmem_limit_bytes=VMEM_LIMIT),
        name="ffn",
    )(x2, gpre, wgate, wup, wdown, gpost)


def _rope_tables(S, half, lo):
    period = 2 * half if lo == 0 else LANES
    inv = ROPE_THETA ** (-jnp.arange(half, dtype=jnp.float32) / half)
    ang = jnp.arange(S, dtype=jnp.float32)[:, None] * inv[None, :]
    cos, sin = jnp.cos(ang), jnp.sin(ang)
    reps = LANES // period
    pad_lo = jnp.zeros((S, lo), jnp.float32)
    pad_hi = jnp.zeros((S, period - lo - 2 * half), jnp.float32)
    zeros = jnp.zeros((S, half), jnp.float32)
    cos_t = jnp.concatenate([pad_lo + 1.0, cos, cos, pad_hi + 1.0], axis=1)
    sin_up = jnp.concatenate([pad_lo, -sin, zeros, pad_hi], axis=1)
    sin_dn = jnp.concatenate([pad_lo, zeros, sin, pad_hi], axis=1)
    return tuple(jnp.tile(t, (1, reps)) for t in (cos_t, sin_up, sin_dn))


def _pad_heads(w, width=HEAD_DIM):
    K = w.shape[0]
    w = w.reshape(K, N_HEADS, width)
    return jnp.pad(w, ((0, 0), (0, 0), (0, LANES - width))).reshape(K, PAD_W)


def _layer_weights(b_forget, w_uq, w_ukv):
    bf = jnp.pad(b_forget, (0, LANES - N_HEADS)).reshape(1, LANES)
    wuq = _pad_heads(w_uq, MLA_NOPE + MLA_ROPE).astype(MXU_DTYPE)
    ukv = w_ukv.reshape(MLA_KV_RANK, N_HEADS, MLA_NOPE + MLA_V)
    wuk = _pad_heads(ukv[:, :, :MLA_NOPE].reshape(MLA_KV_RANK, -1)).astype(MXU_DTYPE)
    wuv = _pad_heads(ukv[:, :, MLA_NOPE:].reshape(MLA_KV_RANK, -1)).astype(MXU_DTYPE)
    return bf, wuq, wuk, wuv


def kernel(x, w_in, b_forget, ret_gn_gain, mla_q_norm, mla_kv_norm, w_uq, w_ukv, w_gate, w_branch, w_out,
           g_pre_mix, g_post_mix, g_pre_ffn, g_post_ffn, w_ffn_gate, w_ffn_up, w_ffn_down):
    B, S, D = x.shape
    depth = w_in.shape[0]
    assert S % ROW_TILE == 0
    ret_tabs = _rope_tables(S, HEAD_DIM // 2, 0)
    mla_tabs = _rope_tables(S, MLA_ROPE // 2, MLA_NOPE)
    r1 = lambda v: v.reshape(1, -1)
    assert w_in.shape[2] == IN_KR_COL + MLA_ROPE
    wg_half = (0.5 * w_gate).astype(MXU_DTYPE)
    wo_half = (0.5 * w_out).astype(MXU_DTYPE)
    wb = w_branch.astype(MXU_DTYPE)
    wf_gate, wf_up, wf_down = (w.astype(MXU_DTYPE) for w in (w_ffn_gate, w_ffn_up, w_ffn_down))
    for l in range(depth):
        bf, wuq, wuk, wuv = _layer_weights(b_forget[l], w_uq[l], w_ukv[l])
        (fq, fk, fva, dq, dk, dva, rq, rk, rv, sg, mq, mk, mva) = _inproj(
            x, r1(g_pre_mix[l]), w_in[l].T, bf, ret_tabs, mla_tabs,
            r1(mla_q_norm[l]), r1(mla_kv_norm[l]), wuq, wuk, wuv)
        o_a = _flash(fq, fk, fva, True)
        o_b = _dilated(dq, dk, dva)
        o_d = _flash(mq, mk, mva, False)
        flat = lambda a: a.reshape(B * S, a.shape[-1])
        x2 = _merge(flat(x), l, S // ROW_TILE, r1(g_pre_mix[l]), flat(o_a), flat(o_b), flat(o_d),
                    flat(rq), flat(rk), flat(rv), flat(sg), r1(ret_gn_gain[l]),
                    wg_half, wb, wo_half, r1(g_post_mix[l]))
        x2 = _ffn(x2, l, r1(g_pre_ffn[l]), wf_gate, wf_up, wf_down, r1(g_post_ffn[l]))
        x = x2.reshape(B, S, D)
    return x
```
